```python
import math
import jax, jax.numpy as jnp
from jax import lax
import numpy as np

D_MODEL = 1024
BATCH = 16
SEQ = 2048
DEPTH = 1

MEM_LEN = 256
ATT_HEADS = 8
ATT_HD = 64
ATT_W = ATT_HEADS * ATT_HD
MOBA_BLOCK = 256
MOBA_TOPK = 3
MOBA_QCHUNK = 16
SSD_W = D_MODEL
SSD_HD = 64
SSD_HEADS = SSD_W // SSD_HD
SSD_GROUPS = 4
SSD_HPG = SSD_HEADS // SSD_GROUPS
SSD_STATE = 128
SSD_CONV = 4
SSD_CHUNK = 256
SSD_XBC = SSD_W + 2 * SSD_GROUPS * SSD_STATE
MEM_HEADS = 4
MEM_HD = 128
MEM_W = MEM_HEADS * MEM_HD
MIX_W = ATT_W + SSD_W + MEM_W
IN_SPLITS = (ATT_W, ATT_W, ATT_W, ATT_W, SSD_W, SSD_XBC, SSD_HEADS, MEM_W, MEM_W)
IN_W = sum(IN_SPLITS)
EPS = 1e-6

kernel_name = "hybrid_moba_ssd_memxattn_block"


def rmsnorm(x, w, eps=EPS):
    xf = x.astype(jnp.float32)
    y = xf * lax.rsqrt(jnp.mean(xf * xf, axis=-1, keepdims=True) + eps)
    return (y * w.astype(jnp.float32)).astype(x.dtype)


def moba_attention(q, k, v):
    bsz, t_len, nh, hd = q.shape
    nb = -(-t_len // MOBA_BLOCK)
    tp = nb * MOBA_BLOCK
    kk = max(1, min(MOBA_TOPK, nb - 1))
    pad = ((0, 0), (0, tp - t_len), (0, 0), (0, 0))
    q, k, v = [jnp.pad(a, pad).transpose(0, 2, 1, 3) for a in (q, k, v)]
    kb = k.reshape(bsz, nh, nb, MOBA_BLOCK, hd)
    vb = v.reshape(bsz, nh, nb, MOBA_BLOCK, hd)
    kmean = jnp.mean(kb.astype(jnp.float32), axis=3)
    qblk = jnp.arange(tp) // MOBA_BLOCK
    gate = jnp.einsum('bhtd,bhnd->bhtn', q.astype(jnp.float32), kmean)
    past = jnp.arange(nb)[None, :] < qblk[:, None]
    gate = jnp.where(past, gate, -jnp.inf)
    _, idx = lax.top_k(gate, kk)
    valid = idx < qblk[:, None]

    nq = tp // MOBA_QCHUNK

    def chunks(a):
        return jnp.moveaxis(a.reshape(bsz, nh, nq, MOBA_QCHUNK, *a.shape[3:]), 2, 0)

    slopes = jnp.exp2(-8.0 * jnp.arange(1, nh + 1, dtype=jnp.float32) / nh)
    scale = hd ** -0.5
    bi = jnp.arange(bsz)[:, None, None, None]
    hi = jnp.arange(nh)[None, :, None, None]
    offs = jnp.arange(MOBA_BLOCK)

    def one_chunk(args):
        c, qc, ic, vc = args
        t = c * MOBA_QCHUNK + jnp.arange(MOBA_QCHUNK)
        ob = (c * MOBA_QCHUNK) // MOBA_BLOCK
        k_own = lax.dynamic_index_in_dim(kb, ob, axis=2, keepdims=False)
        v_own = lax.dynamic_index_in_dim(vb, ob, axis=2, keepdims=False)
        k_sel = kb[bi, hi, ic]
        v_sel = vb[bi, hi, ic]
        s_sel = jnp.einsum('bhqd,bhqjsd->bhqjs', qc, k_sel).astype(jnp.float32) * scale
        pos_sel = ic[..., None] * MOBA_BLOCK + offs
        s_sel = s_sel - slopes[:, None, None, None] * (t[:, None, None] - pos_sel).astype(jnp.float32)
        s_sel = jnp.where(vc[..., None], s_sel, -jnp.inf)
        s_own = jnp.einsum('bhqd,bhsd->bhqs', qc, k_own).astype(jnp.float32) * scale
        dist = t[:, None] - (ob * MOBA_BLOCK + offs)[None, :]
        s_own = jnp.where(dist >= 0, s_own - slopes[:, None, None] * dist.astype(jnp.float32), -jnp.inf)
        logits = jnp.concatenate([s_sel.reshape(bsz, nh, MOBA_QCHUNK, kk * MOBA_BLOCK), s_own], axis=-1)
        p = jax.nn.softmax(logits, axis=-1).astype(qc.dtype)
        p_sel = p[..., :kk * MOBA_BLOCK].reshape(bsz, nh, MOBA_QCHUNK, kk, MOBA_BLOCK)
        p_own = p[..., kk * MOBA_BLOCK:]
        return (jnp.einsum('bhqjs,bhqjsd->bhqd', p_sel, v_sel)
                + jnp.einsum('bhqs,bhsd->bhqd', p_own, v_own))

    out = lax.map(one_chunk, (jnp.arange(nq), chunks(q), chunks(idx), chunks(valid)))
    out = jnp.moveaxis(out, 0, 2).reshape(bsz, nh, tp, hd)[:, :, :t_len]
    return out.transpose(0, 2, 1, 3).reshape(bsz, t_len, nh * hd)


def ssd_chunked(xs, dt, a, bm, cm):
    bsz, t_len = xs.shape[:2]
    nc = t_len // SSD_CHUNK
    xs = xs.reshape(bsz, nc, SSD_CHUNK, SSD_GROUPS, SSD_HPG, SSD_HD)
    dt = dt.reshape(bsz, nc, SSD_CHUNK, SSD_GROUPS, SSD_HPG)
    bm = bm.reshape(bsz, nc, SSD_CHUNK, SSD_GROUPS, SSD_STATE)
    cm = cm.reshape(bsz, nc, SSD_CHUNK, SSD_GROUPS, SSD_STATE)
    da = jnp.moveaxis(dt * a, 2, -1)
    cs = jnp.cumsum(da, axis=-1)
    causal = jnp.tril(jnp.ones((SSD_CHUNK, SSD_CHUNK), dtype=bool))
    seg = cs[..., :, None] - cs[..., None, :]
    lmat = jnp.exp(jnp.where(causal, seg, -jnp.inf))
    xdt = xs * dt[..., None]
    cb = jnp.einsum('bclgn,bcsgn->bcgls', cm, bm)
    y_diag = jnp.einsum('bcgkls,bcsgkp->bclgkp', cb[:, :, :, None] * lmat, xdt)
    decay = jnp.exp(cs[..., -1:] - cs)
    states = jnp.einsum('bclgn,bcgkl,bclgkp->bcgkpn', bm, decay, xdt)
    chunk_decay = jnp.exp(cs[..., -1])

    def step(h, inp):
        st, dec = inp
        return dec[..., None, None] * h + st, h

    h0 = jnp.zeros((bsz, SSD_GROUPS, SSD_HPG, SSD_HD, SSD_STATE), dtype=states.dtype)
    _, prev = lax.scan(step, h0, (jnp.moveaxis(states, 1, 0), jnp.moveaxis(chunk_decay, 1, 0)))
    prev = jnp.moveaxis(prev, 0, 1)
    y_off = jnp.einsum('bclgn,bcgkpn,bcgkl->bclgkp', cm, prev, jnp.exp(cs))
    return (y_diag + y_off).reshape(bsz, t_len, SSD_GROUPS, SSD_HPG, SSD_HD)


def ssd_branch(xbc_raw, dt_raw, z, conv_w, conv_b, dt_bias, a_log, d_skip, norm_w):
    bsz, t_len, _ = xbc_raw.shape
    conv = lax.conv_general_dilated(xbc_raw, conv_w[:, None, :], window_strides=(1,),
                                    padding=[(SSD_CONV - 1, 0)],
                                    dimension_numbers=('NWC', 'WIO', 'NWC'),
                                    feature_group_count=SSD_XBC)
    xbc = jax.nn.silu(conv + conv_b)
    xs, bm, cm = jnp.split(xbc, [SSD_W, SSD_W + SSD_GROUPS * SSD_STATE], axis=-1)
    dt = jax.nn.softplus((dt_raw + dt_bias).astype(jnp.float32))
    a = -jnp.exp(a_log.astype(jnp.float32)).reshape(SSD_GROUPS, SSD_HPG)
    xs = xs.reshape(bsz, t_len, SSD_GROUPS, SSD_HPG, SSD_HD)
    dt = dt.reshape(bsz, t_len, SSD_GROUPS, SSD_HPG)
    bm = bm.reshape(bsz, t_len, SSD_GROUPS, SSD_STATE)
    cm = cm.reshape(bsz, t_len, SSD_GROUPS, SSD_STATE)
    tp = -(-t_len // SSD_CHUNK) * SSD_CHUNK
    padt = lambda arr: jnp.pad(arr, ((0, 0), (0, tp - t_len)) + ((0, 0),) * (arr.ndim - 2))
    y = ssd_chunked(padt(xs), padt(dt), a, padt(bm), padt(cm))[:, :t_len]
    y = (y + d_skip.reshape(SSD_GROUPS, SSD_HPG)[..., None] * xs).astype(xbc_raw.dtype)
    g = (y.reshape(bsz, t_len, SSD_W) * jax.nn.silu(z)).reshape(bsz, t_len, SSD_GROUPS, SSD_W // SSD_GROUPS)
    return rmsnorm(g, norm_w.reshape(SSD_GROUPS, SSD_W // SSD_GROUPS)).reshape(bsz, t_len, SSD_W)


def memory_attention(q, mem_n, w_kv):
    bsz, t_len, _ = q.shape
    kv = mem_n @ w_kv
    k, v = jnp.split(kv, 2, axis=-1)
    k = k.reshape(bsz, -1, MEM_HEADS, MEM_HD)
    v = v.reshape(bsz, -1, MEM_HEADS, MEM_HD)
    q = q.reshape(bsz, t_len, MEM_HEADS, MEM_HD)
    s = jnp.einsum('bthd,bmhd->bhtm', q, k).astype(jnp.float32) * (MEM_HD ** -0.5)
    p = jax.nn.softmax(s, axis=-1).astype(v.dtype)
    return jnp.einsum('bhtm,bmhd->bthd', p, v).reshape(bsz, t_len, MEM_W)


def setup_inputs(seed: int = 0) -> dict:
    key = jax.random.key(seed)
    ks = jax.random.split(key, 16)
    f32 = jnp.float32
    x = jax.random.normal(ks[0], (BATCH, SEQ, D_MODEL), f32)
    mem = jax.random.normal(ks[1], (BATCH, MEM_LEN, D_MODEL), f32)
    norm_w = 1.0 + 0.02 * jax.random.normal(ks[2], (DEPTH, D_MODEL), f32)
    w_in = jax.random.normal(ks[3], (DEPTH, D_MODEL, IN_W), f32) * D_MODEL ** -0.5
    conv_w = jax.random.normal(ks[4], (DEPTH, SSD_CONV, SSD_XBC), f32) * SSD_CONV ** -0.5
    conv_b = 0.02 * jax.random.normal(ks[5], (DEPTH, SSD_XBC), f32)
    u = jax.random.uniform(ks[6], (DEPTH, SSD_HEADS), f32)
    dt0 = jnp.exp(u * (math.log(0.1) - math.log(0.001)) + math.log(0.001))
    dt_bias = dt0 + jnp.log(-jnp.expm1(-dt0))
    a_log = jnp.log(jax.random.uniform(ks[7], (DEPTH, SSD_HEADS), f32, 1.0, 16.0))
    d_skip = 1.0 + 0.1 * jax.random.normal(ks[8], (DEPTH, SSD_HEADS), f32)
    ssd_norm_w = 1.0 + 0.02 * jax.random.normal(ks[9], (DEPTH, SSD_W), f32)
    mem_norm_w = 1.0 + 0.02 * jax.random.normal(ks[10], (DEPTH, D_MODEL), f32)
    w_mem_kv = jax.random.normal(ks[11], (DEPTH, D_MODEL, 2 * MEM_W), f32) * D_MODEL ** -0.5
    w_out = jax.random.normal(ks[12], (DEPTH, MIX_W, D_MODEL), f32) * MIX_W ** -0.5
    final_norm_w = 1.0 + 0.02 * jax.random.normal(ks[13], (D_MODEL,), f32)
    return {"x": x, "mem": mem, "norm_w": norm_w, "w_in": w_in, "conv_w": conv_w,
            "conv_b": conv_b, "dt_bias": dt_bias, "a_log": a_log, "d_skip": d_skip,
            "ssd_norm_w": ssd_norm_w, "mem_norm_w": mem_norm_w, "w_mem_kv": w_mem_kv,
            "w_out": w_out, "final_norm_w": final_norm_w}


def reference(x, mem, norm_w, w_in, conv_w, conv_b, dt_bias, a_log, d_skip,
              ssd_norm_w, mem_norm_w, w_mem_kv, w_out, final_norm_w):
    bsz, t_len, _ = x.shape
    offsets = list(np.cumsum(IN_SPLITS)[:-1])
    h = x
    for l in range(DEPTH):
        u = rmsnorm(h, norm_w[l])
        proj = u @ w_in[l]
        q_a, k_a, v_a, g_a, z_s, xbc_s, dt_s, q_m, g_m = jnp.split(proj, offsets, axis=-1)
        shp = (bsz, t_len, ATT_HEADS, ATT_HD)
        o_att = moba_attention(q_a.reshape(shp), k_a.reshape(shp), v_a.reshape(shp)) * jax.nn.silu(g_a)
        o_ssd = ssd_branch(xbc_s, dt_s, z_s, conv_w[l], conv_b[l], dt_bias[l], a_log[l],
                           d_skip[l], ssd_norm_w[l])
        mem_n = rmsnorm(mem, mem_norm_w[l])
        o_mem = memory_attention(q_m, mem_n, w_mem_kv[l]) * jax.nn.silu(g_m)
        mixed = jnp.concatenate([o_att, o_ssd, o_mem], axis=-1)
        h = h + mixed @ w_out[l]
    return rmsnorm(h, final_norm_w)
```

```python
import functools

import jax
import jax.numpy as jnp
from jax import lax
from jax.experimental import pallas as pl
from jax.experimental.pallas import tpu as pltpu

F32 = jnp.float32
BF16 = jnp.bfloat16

D_MODEL = 1024
ATT_HEADS = 8
ATT_HD = 64
ATT_W = ATT_HEADS * ATT_HD
ATT_PAIRS = ATT_HEADS // 2
MOBA_BLOCK = 256
MOBA_TOPK = 3
SSD_W = 1024
SSD_HD = 64
SSD_HEADS = 16
SSD_GROUPS = 4
SSD_HPG = SSD_HEADS // SSD_GROUPS
SSD_GW = SSD_W // SSD_GROUPS
SSD_STATE = 128
SSD_CONV = 4
SSD_CHUNK = 256
SSD_XBC = SSD_W + 2 * SSD_GROUPS * SSD_STATE
MEM_HEADS = 4
MEM_HD = 128
MEM_W = MEM_HEADS * MEM_HD
MIX_W = ATT_W + SSD_W + MEM_W
EPS = 1e-6
NEG = -1e30
LANES = 128
MAIN_W = 4 * ATT_W + SSD_W + SSD_XBC + 2 * MEM_W
VMEM_LIMIT = 56 * 1024 * 1024

_NT = (((1,), (1,)), ((), ()))


def _split3(a):
    hi = a.astype(BF16)
    r1 = a - hi.astype(F32)
    mid = r1.astype(BF16)
    lo = (r1 - mid.astype(F32)).astype(BF16)
    return hi, mid, lo


def _silu(a):
    return a * jax.nn.sigmoid(a)


def _inproj_kernel(x_ref, nw_ref, w_ref, wdt_ref,
                   qkvg_ref, z_ref, xs_ref, bm_ref, cm_ref, qm_ref, gm_ref, dt_ref):
    x = x_ref[...]
    ms = jnp.mean(x * x, axis=-1, keepdims=True)
    u = (x * lax.rsqrt(ms + EPS) * nw_ref[...]).astype(BF16)

    def cols(c0, width):
        return jnp.dot(u, w_ref[:, c0:c0 + width], preferred_element_type=F32).astype(BF16)

    def scatter(ref, c0, piece, count):
        for c in range(0, piece * count, 512):
            r = cols(c0 + c, 512)
            for j in range(512 // piece):
                ref[(c + j * piece) // piece] = r[:, j * piece:(j + 1) * piece]

    scatter(qkvg_ref, 0, LANES, 16)
    scatter(z_ref, 4 * ATT_W, SSD_GW, SSD_GROUPS)
    scatter(xs_ref, 4 * ATT_W + SSD_W, SSD_GW, SSD_GROUPS)
    scatter(bm_ref, 4 * ATT_W + 2 * SSD_W, SSD_STATE, SSD_GROUPS)
    scatter(cm_ref, 4 * ATT_W + 2 * SSD_W + SSD_GROUPS * SSD_STATE, SSD_STATE, SSD_GROUPS)
    scatter(qm_ref, 4 * ATT_W + SSD_W + SSD_XBC, MEM_HD, MEM_HEADS)
    scatter(gm_ref, 4 * ATT_W + SSD_W + SSD_XBC + MEM_W, MEM_HD, MEM_HEADS)
    dt_ref[...] = jnp.dot(u, wdt_ref[...], preferred_element_type=F32)


def _in_proj(x2d, norm_w, w_main, w_dt, tm):
    m = x2d.shape[0]
    const = lambda i: (0, 0)
    lead = lambda i: (0, i, 0)
    out_shape = (
        jax.ShapeDtypeStruct((16, m, LANES), BF16),
        jax.ShapeDtypeStruct((SSD_GROUPS, m, SSD_GW), BF16),
        jax.ShapeDtypeStruct((SSD_GROUPS, m, SSD_GW), BF16),
        jax.ShapeDtypeStruct((SSD_GROUPS, m, SSD_STATE), BF16),
        jax.ShapeDtypeStruct((SSD_GROUPS, m, SSD_STATE), BF16),
        jax.ShapeDtypeStruct((MEM_HEADS, m, MEM_HD), BF16),
        jax.ShapeDtypeStruct((MEM_HEADS, m, MEM_HD), BF16),
        jax.ShapeDtypeStruct((m, LANES), F32),
    )
    out_specs = (
        pl.BlockSpec((16, tm, LANES), lead),
        pl.BlockSpec((SSD_GROUPS, tm, SSD_GW), lead),
        pl.BlockSpec((SSD_GROUPS, tm, SSD_GW), lead),
        pl.BlockSpec((SSD_GROUPS, tm, SSD_STATE), lead),
        pl.BlockSpec((SSD_GROUPS, tm, SSD_STATE), lead),
        pl.BlockSpec((MEM_HEADS, tm, MEM_HD), lead),
        pl.BlockSpec((MEM_HEADS, tm, MEM_HD), lead),
        pl.BlockSpec((tm, LANES), lambda i: (i, 0)),
    )
    return pl.pallas_call(
        _inproj_kernel,
        grid=(m // tm,),
        in_specs=[
            pl.BlockSpec((tm, D_MODEL), lambda i: (i, 0)),
            pl.BlockSpec((1, D_MODEL), const),
            pl.BlockSpec((D_MODEL, MAIN_W), const, pipeline_mode=pl.Buffered(1)),
            pl.BlockSpec((D_MODEL, LANES), const, pipeline_mode=pl.Buffered(1)),
        ],
        out_specs=out_specs,
        out_shape=out_shape,
        compiler_params=pltpu.CompilerParams(
            dimension_semantics=("arbitrary",), vmem_limit_bytes=VMEM_LIMIT),
        name="in_proj",
    )(x2d, norm_w, w_main, w_dt)


def _moba_kernel(q_ref, k_ref, v_ref, g_ref, kc_ref, o_ref):
    t_len = q_ref.shape[0]
    nb = t_len // MOBA_BLOCK
    half = LANES // 2

    q = q_ref[...]
    k = k_ref[...]
    v = v_ref[...]
    kc = kc_ref[...]
    lo = lax.broadcasted_iota(jnp.int32, (t_len, LANES), 1) < half

    kmean = jnp.mean(k.astype(F32).reshape(nb, MOBA_BLOCK, LANES), axis=1)
    lo8 = lax.broadcasted_iota(jnp.int32, (nb, LANES), 1) < half
    terms = []
    for km in (jnp.where(lo8, kmean, 0.0), jnp.where(lo8, 0.0, kmean)):
        hi = km.astype(BF16).astype(F32)
        terms += [hi, km - hi]
    kstack = jnp.concatenate(terms, axis=0).astype(BF16)
    gate_t = lax.dot_general(kstack, q, _NT, preferred_element_type=F32)

    tpos = lax.broadcasted_iota(jnp.int32, (nb, t_len), 1)
    qblk = tpos // MOBA_BLOCK
    jidx = lax.broadcasted_iota(jnp.int32, (nb, t_len), 0)

    def select_bias(gate):
        cnt = jnp.zeros((nb, t_len), jnp.int32)
        for i in range(nb):
            gi = gate[i:i + 1, :]
            beats = (gi > gate) | ((gi == gate) & (i < jidx))
            cnt = cnt + jnp.where(beats & (i < qblk), 1, 0)
        sel = ((cnt < MOBA_TOPK) & (jidx < qblk)) | (jidx == qblk)
        return jnp.where(sel, 0.0, NEG)

    bias0 = select_bias(gate_t[0:nb] + gate_t[nb:2 * nb])
    bias1 = select_bias(gate_t[2 * nb:3 * nb] + gate_t[3 * nb:4 * nb])
    ones2 = (lax.broadcasted_iota(jnp.int32, (8, t_len), 0) < 2).astype(F32)
    pad = jnp.zeros((half - nb - 8, t_len), F32)
    aug = jnp.concatenate([bias1, ones2, pad, bias0, ones2, pad], axis=0).T.astype(BF16)

    qs = q * jnp.asarray(ATT_HD ** -0.5, BF16)
    qa = (jnp.where(lo, qs, aug), jnp.where(lo, aug, qs))
    ka = (jnp.where(lo, k, kc), jnp.where(lo, kc, k))

    row = lax.broadcasted_iota(jnp.int32, (MOBA_BLOCK, MOBA_BLOCK), 0)
    col = lax.broadcasted_iota(jnp.int32, (MOBA_BLOCK, MOBA_BLOCK), 1)
    causal = col <= row
    lo_blk = lax.broadcasted_iota(jnp.int32, (MOBA_BLOCK, LANES), 1) < half

    for n in range(nb):
        r0 = n * MOBA_BLOCK
        width = r0 + MOBA_BLOCK
        outs = []
        for h in range(2):
            s = lax.dot_general(qa[h][r0:width], ka[h][:width], _NT, preferred_element_type=F32)
            s_diag = jnp.where(causal, s[:, r0:], NEG)
            s = s_diag if n == 0 else jnp.concatenate([s[:, :r0], s_diag], axis=1)
            m = jnp.max(s, axis=-1, keepdims=True)
            p = jnp.exp(s - m)
            l = jnp.sum(p, axis=-1, keepdims=True)
            o = jnp.dot(p.astype(BF16), v[:width], preferred_element_type=F32)
            outs.append(o / l)
        o = jnp.where(lo_blk, outs[0], outs[1])
        g = g_ref[r0:width, :].astype(F32)
        o_ref[r0:width, :] = (o * _silu(g)).astype(BF16)


def _moba(qkvg, kc, batch, t_len):
    m = batch * t_len
    blk = lambda kind: pl.BlockSpec((None, t_len, LANES), lambda b, hp: (kind * ATT_PAIRS + hp, b, 0))
    return pl.pallas_call(
        _moba_kernel,
        grid=(batch, ATT_PAIRS),
        in_specs=[blk(0), blk(1), blk(2), blk(3),
                  pl.BlockSpec((None, t_len, LANES), lambda b, hp: (hp, 0, 0))],
        out_specs=pl.BlockSpec((None, t_len, LANES), lambda b, hp: (hp, b, 0)),
        out_shape=jax.ShapeDtypeStruct((ATT_PAIRS, m, LANES), BF16),
        compiler_params=pltpu.CompilerParams(
            dimension_semantics=("arbitrary", "arbitrary"), vmem_limit_bytes=VMEM_LIMIT),
        name="moba",
    )(qkvg, qkvg, qkvg, qkvg, kc)


def _moba_key_table(t_len):
    pos = jnp.arange(t_len)
    blk = pos // MOBA_BLOCK
    off = pos % MOBA_BLOCK
    slopes = jnp.exp2(-8.0 * jnp.arange(1, ATT_HEADS + 1, dtype=F32) / ATT_HEADS)
    onehot = (blk[:, None] == jnp.arange(8)[None, :]).astype(F32)
    half = LANES // 2

    def head_table(h):
        cols = [onehot, (slopes[h] * off.astype(F32))[:, None],
                (slopes[h] * MOBA_BLOCK * blk.astype(F32))[:, None],
                jnp.zeros((t_len, half - 10), F32)]
        return jnp.concatenate(cols, axis=1)

    tables = [jnp.concatenate([head_table(2 * hp + 1), head_table(2 * hp)], axis=1)
              for hp in range(ATT_PAIRS)]
    return jnp.stack(tables).astype(BF16)


def _ssd_kernel(xs_ref, bm_ref, cm_ref, z_ref, dt_ref,
                cwx_ref, cwb_ref, cwc_ref, cbx_ref, cbb_ref, cbc_ref,
                dtb_ref, alog_ref, dskip_ref, nw_ref,
                o_ref, xs_s, bm_s, cm_s, dte_s):
    grp = pl.program_id(1)
    t_len = xs_ref.shape[0]
    nchunks = t_len // SSD_CHUNK

    def conv_silu(x_ref, w_ref, b_ref):
        x = x_ref[...].astype(F32)
        w = w_ref[...]
        rows = lax.broadcasted_iota(jnp.int32, x.shape, 0)
        acc = x * w[SSD_CONV - 1:SSD_CONV, :] + b_ref[...]
        for s in range(1, SSD_CONV):
            shifted = jnp.where(rows >= s, pltpu.roll(x, s, axis=0), 0.0)
            acc = acc + shifted * w[SSD_CONV - 1 - s:SSD_CONV - s, :]
        return _silu(acc)

    xs_s[...] = conv_silu(xs_ref, cwx_ref, cbx_ref)
    bm_s[...] = conv_silu(bm_ref, cwb_ref, cbb_ref)
    cm_s[...] = conv_silu(cm_ref, cwc_ref, cbc_ref)

    dt = jnp.logaddexp(dt_ref[...] + dtb_ref[...], 0.0)
    hrow = lax.broadcasted_iota(jnp.int32, (LANES, SSD_GW), 0)
    lcol = lax.broadcasted_iota(jnp.int32, (LANES, SSD_GW), 1)
    expand = (hrow == SSD_HPG * grp + lcol // SSD_HD).astype(BF16)
    dte = None
    for term in _split3(dt):
        part = jnp.dot(term, expand, preferred_element_type=F32)
        dte = part if dte is None else dte + part
    dte_s[...] = dte

    a_e = -jnp.exp(alog_ref[...])
    d_e = dskip_ref[...]
    nw = nw_ref[...]
    li = lax.broadcasted_iota(jnp.int32, (SSD_CHUNK, SSD_CHUNK), 0)
    si = lax.broadcasted_iota(jnp.int32, (SSD_CHUNK, SSD_CHUNK), 1)
    tri = li >= si
    lmat = tri.astype(BF16)
    head_of_lane = lax.broadcasted_iota(jnp.int32, (SSD_CHUNK, SSD_GW), 1) // SSD_HD

    def chunk(c, h_t):
        r0 = pl.multiple_of(c * SSD_CHUNK, SSD_CHUNK)
        rows = pl.ds(r0, SSD_CHUNK)
        xs = xs_s[rows, :]
        bm = bm_s[rows, :]
        cm = cm_s[rows, :]
        dte_c = dte_s[rows, :]
        da = dte_c * a_e
        cs = None
        for term in _split3(da):
            part = jnp.dot(lmat, term, preferred_element_type=F32)
            cs = part if cs is None else cs + part
        cs_t = cs.T
        xdt = xs * dte_c
        xdt_b = xdt.astype(BF16)
        cm_b = cm.astype(BF16)
        cb = lax.dot_general(cm_b, bm.astype(BF16), _NT, preferred_element_type=F32)
        ws, xm = [], []
        for k in range(SSD_HPG):
            colb = jnp.broadcast_to(cs[:, SSD_HD * k:SSD_HD * k + 1], (SSD_CHUNK, SSD_CHUNK))
            rowb = cs_t[SSD_HD * k:SSD_HD * k + 1, :]
            decay = jnp.exp(jnp.where(tri, colb - rowb, NEG))
            ws.append((cb * decay).astype(BF16))
            xm.append(jnp.where(head_of_lane == k, xdt_b, jnp.zeros_like(xdt_b)))
        y = jnp.dot(jnp.concatenate(ws, axis=1), jnp.concatenate(xm, axis=0),
                    preferred_element_type=F32)
        cs_last = cs[SSD_CHUNK - 1:SSD_CHUNK, :]
        state_in = (xdt * jnp.exp(cs_last - cs)).astype(BF16)
        st = jnp.dot(bm.T.astype(BF16), state_in, preferred_element_type=F32)
        y = y + jnp.dot(cm_b, h_t.astype(BF16), preferred_element_type=F32) * jnp.exp(cs)
        y = y + d_e * xs
        z = z_ref[rows, :].astype(F32)
        gated = y * _silu(z)
        var = jnp.mean(gated * gated, axis=-1, keepdims=True)
        o_ref[rows, :] = (gated * lax.rsqrt(var + EPS) * nw).astype(BF16)
        return h_t * jnp.exp(cs_last) + st

    lax.fori_loop(0, nchunks, chunk, jnp.zeros((SSD_STATE, SSD_GW), F32))


def _ssd(xs, bm, cm, z, dt, conv_w, conv_b, dt_bias, a_log, d_skip, norm_w, batch, t_len):
    m = batch * t_len
    g_blk = lambda w: pl.BlockSpec((None, t_len, w), lambda b, g: (g, b, 0))
    g_par = lambda r, w: pl.BlockSpec((None, r, w), lambda b, g: (g, 0, 0))
    cw_x = conv_w[:, :SSD_W].reshape(SSD_CONV, SSD_GROUPS, SSD_GW).transpose(1, 0, 2)
    cw_b = conv_w[:, SSD_W:SSD_W + SSD_GROUPS * SSD_STATE].reshape(SSD_CONV, SSD_GROUPS, SSD_STATE).transpose(1, 0, 2)
    cw_c = conv_w[:, SSD_W + SSD_GROUPS * SSD_STATE:].reshape(SSD_CONV, SSD_GROUPS, SSD_STATE).transpose(1, 0, 2)
    cb_x = conv_b[:SSD_W].reshape(SSD_GROUPS, 1, SSD_GW)
    cb_b = conv_b[SSD_W:SSD_W + SSD_GROUPS * SSD_STATE].reshape(SSD_GROUPS, 1, SSD_STATE)
    cb_c = conv_b[SSD_W + SSD_GROUPS * SSD_STATE:].reshape(SSD_GROUPS, 1, SSD_STATE)
    dtb = jnp.pad(dt_bias, (0, LANES - SSD_HEADS)).reshape(1, LANES)
    per_lane = lambda a: jnp.repeat(a, SSD_HD).reshape(SSD_GROUPS, 1, SSD_GW)
    return pl.pallas_call(
        _ssd_kernel,
        grid=(batch, SSD_GROUPS),
        in_specs=[g_blk(SSD_GW), g_blk(SSD_STATE), g_blk(SSD_STATE), g_blk(SSD_GW),
                  pl.BlockSpec((t_len, LANES), lambda b, g: (b, 0)),
                  g_par(SSD_CONV, SSD_GW), g_par(SSD_CONV, SSD_STATE), g_par(SSD_CONV, SSD_STATE),
                  g_par(1, SSD_GW), g_par(1, SSD_STATE), g_par(1, SSD_STATE),
                  pl.BlockSpec((1, LANES), lambda b, g: (0, 0)),
                  g_par(1, SSD_GW), g_par(1, SSD_GW), g_par(1, SSD_GW)],
        out_specs=g_blk(SSD_GW),
        out_shape=jax.ShapeDtypeStruct((SSD_GROUPS, m, SSD_GW), BF16),
        scratch_shapes=[pltpu.VMEM((t_len, SSD_GW), F32), pltpu.VMEM((t_len, SSD_STATE), F32),
                        pltpu.VMEM((t_len, SSD_STATE), F32), pltpu.VMEM((t_len, SSD_GW), F32)],
        compiler_params=pltpu.CompilerParams(
            dimension_semantics=("arbitrary", "arbitrary"), vmem_limit_bytes=VMEM_LIMIT),
        name="ssd",
    )(xs, bm, cm, z, dt, cw_x, cw_b, cw_c, cb_x, cb_b, cb_c, dtb,
      per_lane(a_log), per_lane(d_skip), norm_w.reshape(SSD_GROUPS, 1, SSD_GW))


def _mem_kernel(mem_ref, mnw_ref, wkv_ref, q_ref, g_ref, o_ref, *, q_chunk):
    t_len = q_ref.shape[1]
    mem = mem_ref[...]
    ms = jnp.mean(mem * mem, axis=-1, keepdims=True)
    mem_n = (mem * lax.rsqrt(ms + EPS) * mnw_ref[...]).astype(BF16)
    kv = jnp.dot(mem_n, wkv_ref[...], preferred_element_type=F32).astype(BF16)
    scale = MEM_HD ** -0.5
    for h in range(MEM_HEADS):
        k = kv[:, h * MEM_HD:(h + 1) * MEM_HD]
        v = kv[:, MEM_W + h * MEM_HD:MEM_W + (h + 1) * MEM_HD]
        for r0 in range(0, t_len, q_chunk):
            s = lax.dot_general(q_ref[h, r0:r0 + q_chunk, :], k, _NT, preferred_element_type=F32) * scale
            m = jnp.max(s, axis=-1, keepdims=True)
            p = jnp.exp(s - m)
            l = jnp.sum(p, axis=-1, keepdims=True)
            o = jnp.dot(p.astype(BF16), v, preferred_element_type=F32) / l
            g = g_ref[h, r0:r0 + q_chunk, :].astype(F32)
            o_ref[h, r0:r0 + q_chunk, :] = (o * _silu(g)).astype(BF16)


def _mem_attn(mem2d, mem_norm_w, w_kv, qm, gm, batch, t_len, mem_len):
    m = batch * t_len
    return pl.pallas_call(
        functools.partial(_mem_kernel, q_chunk=512),
        grid=(batch,),
        in_specs=[pl.BlockSpec((mem_len, D_MODEL), lambda b: (b, 0)),
                  pl.BlockSpec((1, D_MODEL), lambda b: (0, 0)),
                  pl.BlockSpec((D_MODEL, 2 * MEM_W), lambda b: (0, 0), pipeline_mode=pl.Buffered(1)),
                  pl.BlockSpec((MEM_HEADS, t_len, MEM_HD), lambda b: (0, b, 0)),
                  pl.BlockSpec((MEM_HEADS, t_len, MEM_HD), lambda b: (0, b, 0))],
        out_specs=pl.BlockSpec((MEM_HEADS, t_len, MEM_HD), lambda b: (0, b, 0)),
        out_shape=jax.ShapeDtypeStruct((MEM_HEADS, m, MEM_HD), BF16),
        compiler_params=pltpu.CompilerParams(
            dimension_semantics=("arbitrary",), vmem_limit_bytes=VMEM_LIMIT),
        name="mem_attn",
    )(mem2d, mem_norm_w, w_kv, qm, gm)


def _outproj_kernel(oatt_ref, ossd_ref, omem_ref, x_ref, w_ref, fnw_ref, o_ref):
    pieces = ([oatt_ref[i] for i in range(ATT_PAIRS)] + [ossd_ref[i] for i in range(SSD_GROUPS)]
              + [omem_ref[i] for i in range(MEM_HEADS)])
    mixed = jnp.concatenate(pieces, axis=-1)
    h = x_ref[...] + jnp.dot(mixed, w_ref[...], preferred_element_type=F32)
    ms = jnp.mean(h * h, axis=-1, keepdims=True)
    o_ref[...] = h * lax.rsqrt(ms + EPS) * fnw_ref[...]


def _out_proj(oatt, ossd, omem, x2d, w_out, final_norm_w, tm):
    m = x2d.shape[0]
    lead = lambda i: (0, i, 0)
    return pl.pallas_call(
        _outproj_kernel,
        grid=(m // tm,),
        in_specs=[pl.BlockSpec((ATT_PAIRS, tm, LANES), lead),
                  pl.BlockSpec((SSD_GROUPS, tm, SSD_GW), lead),
                  pl.BlockSpec((MEM_HEADS, tm, MEM_HD), lead),
                  pl.BlockSpec((tm, D_MODEL), lambda i: (i, 0)),
                  pl.BlockSpec((MIX_W, D_MODEL), lambda i: (0, 0), pipeline_mode=pl.Buffered(1)),
                  pl.BlockSpec((1, D_MODEL), lambda i: (0, 0))],
        out_specs=pl.BlockSpec((tm, D_MODEL), lambda i: (i, 0)),
        out_shape=jax.ShapeDtypeStruct((m, D_MODEL), F32),
        compiler_params=pltpu.CompilerParams(
            dimension_semantics=("arbitrary",), vmem_limit_bytes=VMEM_LIMIT),
        name="out_proj",
    )(oatt, ossd, omem, x2d, w_out, final_norm_w)


def kernel(x, mem, norm_w, w_in, conv_w, conv_b, dt_bias, a_log, d_skip, ssd_norm_w,
           mem_norm_w, w_mem_kv, w_out, final_norm_w):
    batch, t_len, d_model = x.shape
    mem_len = mem.shape[1]
    depth = norm_w.shape[0]
    assert d_model == D_MODEL and depth == 1 and t_len % MOBA_BLOCK == 0 and t_len // MOBA_BLOCK == 8
    dt0 = 4 * ATT_W + SSD_W + SSD_XBC
    x2d = x.reshape(batch * t_len, D_MODEL)
    mem2d = mem.reshape(batch * mem_len, D_MODEL)
    w = w_in[0]
    w_main = jnp.concatenate([w[:, :dt0], w[:, dt0 + SSD_HEADS:]], axis=1).astype(BF16)
    w_dt = jnp.pad(w[:, dt0:dt0 + SSD_HEADS], ((0, 0), (0, LANES - SSD_HEADS))).astype(BF16)

    qkvg, z, xs, bm, cm, qm, gm, dt = _in_proj(x2d, norm_w[0].reshape(1, D_MODEL), w_main, w_dt, tm=512)
    o_att = _moba(qkvg, _moba_key_table(t_len), batch, t_len)
    o_ssd = _ssd(xs, bm, cm, z, dt, conv_w[0], conv_b[0], dt_bias[0], a_log[0], d_skip[0],
                 ssd_norm_w[0], batch, t_len)
    o_mem = _mem_attn(mem2d, mem_norm_w[0].reshape(1, D_MODEL), w_mem_kv[0].astype(BF16), qm, gm,
                      batch, t_len, mem_len)
    out = _out_proj(o_att, o_ssd, o_mem, x2d, w_out[0].astype(BF16), final_norm_w.reshape(1, D_MODEL), tm=512)
    return out.reshape(batch, t_len, D_MODEL)
```

```python
import functools

import jax
import jax.numpy as jnp
from jax import lax
from jax.experimental import pallas as pl
from jax.experimental.pallas import tpu as pltpu

F32 = jnp.float32
BF16 = jnp.bfloat16

D_MODEL = 1024
ATT_HEADS = 8
ATT_HD = 64
ATT_W = ATT_HEADS * ATT_HD
ATT_PAIRS = ATT_HEADS // 2
MOBA_BLOCK = 256
MOBA_TOPK = 3
SSD_W = 1024
SSD_HD = 64
SSD_HEADS = 16
SSD_GROUPS = 4
SSD_HPG = SSD_HEADS // SSD_GROUPS
SSD_GW = SSD_W // SSD_GROUPS
SSD_STREAMS = 2
SSD_STATE = 128
SSD_CONV = 4
SSD_CHUNK = 256
SSD_XBC = SSD_W + 2 * SSD_GROUPS * SSD_STATE
MEM_HEADS = 4
MEM_HD = 128
MEM_W = MEM_HEADS * MEM_HD
MIX_W = ATT_W + SSD_W + MEM_W
EPS = 1e-6
NEG = -1e30
LOG2E = 1.4426950408889634
_LOG2E_TERMS = (1.4453125, -0.00262451171875, 7.063150405883789e-06)
LANES = 128
MAIN_W = 4 * ATT_W + SSD_W + SSD_XBC + 2 * MEM_W
VMEM_LIMIT = 56 * 1024 * 1024

_NT = (((1,), (1,)), ((), ()))


def _split3(a):
    hi = a.astype(BF16)
    r1 = a - hi.astype(F32)
    mid = r1.astype(BF16)
    lo = (r1 - mid.astype(F32)).astype(BF16)
    return hi, mid, lo


def _silu(a):
    return a * jax.nn.sigmoid(a)


def _pipeline(gens):
    live = []
    pending = list(gens)
    while pending or live:
        if pending:
            live.append(pending.pop(0))
        for g in list(reversed(live)):
            try:
                next(g)
            except StopIteration:
                live.remove(g)


def _round_robin(gens):
    results = [None] * len(gens)
    active = list(range(len(gens)))
    while active:
        for j in list(active):
            try:
                next(gens[j])
            except StopIteration as done:
                results[j] = done.value
                active.remove(j)
    return tuple(results)


def _inproj_kernel(x_ref, nw_ref, w_ref, wdt_ref,
                   qkvg_ref, z_ref, xs_ref, bm_ref, cm_ref, qm_ref, gm_ref, dt_ref):
    x = x_ref[...]
    ms = jnp.mean(x * x, axis=-1, keepdims=True)
    u = (x * lax.rsqrt(ms + EPS) * nw_ref[...]).astype(BF16)

    def cols(c0, width):
        return jnp.dot(u, w_ref[:, c0:c0 + width], preferred_element_type=F32).astype(BF16)

    def scatter(ref, c0, piece, count):
        for c in range(0, piece * count, 512):
            r = cols(c0 + c, 512)
            for j in range(512 // piece):
                ref[(c + j * piece) // piece] = r[:, j * piece:(j + 1) * piece]

    scatter(qkvg_ref, 0, LANES, 16)
    scatter(z_ref, 4 * ATT_W, SSD_GW, SSD_GROUPS)
    scatter(xs_ref, 4 * ATT_W + SSD_W, SSD_GW, SSD_GROUPS)
    scatter(bm_ref, 4 * ATT_W + 2 * SSD_W, SSD_STATE, SSD_GROUPS)
    scatter(cm_ref, 4 * ATT_W + 2 * SSD_W + SSD_GROUPS * SSD_STATE, SSD_STATE, SSD_GROUPS)
    scatter(qm_ref, 4 * ATT_W + SSD_W + SSD_XBC, MEM_HD, MEM_HEADS)
    scatter(gm_ref, 4 * ATT_W + SSD_W + SSD_XBC + MEM_W, MEM_HD, MEM_HEADS)
    dt_ref[...] = jnp.dot(u, wdt_ref[...], preferred_element_type=F32)


def _in_proj(x2d, norm_w, w_main, w_dt, tm):
    m = x2d.shape[0]
    const = lambda i: (0, 0)
    lead = lambda i: (0, i, 0)
    out_shape = (
        jax.ShapeDtypeStruct((16, m, LANES), BF16),
        jax.ShapeDtypeStruct((SSD_GROUPS, m, SSD_GW), BF16),
        jax.ShapeDtypeStruct((SSD_GROUPS, m, SSD_GW), BF16),
        jax.ShapeDtypeStruct((SSD_GROUPS, m, SSD_STATE), BF16),
        jax.ShapeDtypeStruct((SSD_GROUPS, m, SSD_STATE), BF16),
        jax.ShapeDtypeStruct((MEM_HEADS, m, MEM_HD), BF16),
        jax.ShapeDtypeStruct((MEM_HEADS, m, MEM_HD), BF16),
        jax.ShapeDtypeStruct((m, LANES), F32),
    )
    out_specs = (
        pl.BlockSpec((16, tm, LANES), lead),
        pl.BlockSpec((SSD_GROUPS, tm, SSD_GW), lead),
        pl.BlockSpec((SSD_GROUPS, tm, SSD_GW), lead),
        pl.BlockSpec((SSD_GROUPS, tm, SSD_STATE), lead),
        pl.BlockSpec((SSD_GROUPS, tm, SSD_STATE), lead),
        pl.BlockSpec((MEM_HEADS, tm, MEM_HD), lead),
        pl.BlockSpec((MEM_HEADS, tm, MEM_HD), lead),
        pl.BlockSpec((tm, LANES), lambda i: (i, 0)),
    )
    return pl.pallas_call(
        _inproj_kernel,
        grid=(m // tm,),
        in_specs=[
            pl.BlockSpec((tm, D_MODEL), lambda i: (i, 0)),
            pl.BlockSpec((1, D_MODEL), const),
            pl.BlockSpec((D_MODEL, MAIN_W), const, pipeline_mode=pl.Buffered(1)),
            pl.BlockSpec((D_MODEL, LANES), const, pipeline_mode=pl.Buffered(1)),
        ],
        out_specs=out_specs,
        out_shape=out_shape,
        compiler_params=pltpu.CompilerParams(
            dimension_semantics=("arbitrary",), vmem_limit_bytes=VMEM_LIMIT),
        name="in_proj",
    )(x2d, norm_w, w_main, w_dt)


def _moba_kernel(q_ref, k_ref, v_ref, g_ref, kc_ref, o_ref):
    t_len = q_ref.shape[0]
    nb = t_len // MOBA_BLOCK
    half = LANES // 2

    q = q_ref[...]
    k = k_ref[...]
    v = v_ref[...]
    kc = kc_ref[...]
    lo = lax.broadcasted_iota(jnp.int32, (t_len, LANES), 1) < half

    kmean = jnp.mean(k.astype(F32).reshape(nb, MOBA_BLOCK, LANES), axis=1)
    lo8 = lax.broadcasted_iota(jnp.int32, (nb, LANES), 1) < half
    terms = []
    for km in (jnp.where(lo8, kmean, 0.0), jnp.where(lo8, 0.0, kmean)):
        hi = km.astype(BF16).astype(F32)
        terms += [hi, km - hi]
    kstack = jnp.concatenate(terms, axis=0).astype(BF16)
    gate_t = lax.dot_general(kstack, q, _NT, preferred_element_type=F32)

    tpos = lax.broadcasted_iota(jnp.int32, (nb, t_len), 1)
    qblk = tpos // MOBA_BLOCK
    jidx = lax.broadcasted_iota(jnp.int32, (nb, t_len), 0)

    def select_bias(gate):
        cnt = jnp.zeros((nb, t_len), jnp.int32)
        for i in range(nb):
            gi = gate[i:i + 1, :]
            beats = (gi > gate) | ((gi == gate) & (i < jidx))
            cnt = cnt + jnp.where(beats & (i < qblk), 1, 0)
        sel = ((cnt < MOBA_TOPK) & (jidx < qblk)) | (jidx == qblk)
        return jnp.where(sel, 0.0, NEG)

    bias0 = select_bias(gate_t[0:nb] + gate_t[nb:2 * nb])
    bias1 = select_bias(gate_t[2 * nb:3 * nb] + gate_t[3 * nb:4 * nb])
    crow = lax.broadcasted_iota(jnp.int32, (8, t_len), 0)
    l2e = jnp.zeros((8, t_len), F32)
    for i, term in enumerate(_LOG2E_TERMS + _LOG2E_TERMS):
        l2e = jnp.where(crow == i, term, l2e)
    pad = jnp.zeros((half - nb - 8, t_len), F32)
    aug = jnp.concatenate([bias1, l2e, pad, bias0, l2e, pad], axis=0).T.astype(BF16)

    qs = (q.astype(F32) * (ATT_HD ** -0.5 * LOG2E)).astype(BF16)
    qa = (jnp.where(lo, qs, aug), jnp.where(lo, aug, qs))
    ka = (jnp.where(lo, k, kc), jnp.where(lo, kc, k))

    v_t = v.astype(F32).T.astype(BF16)
    ones_rows = (lax.broadcasted_iota(jnp.int32, (16, t_len), 0) == 0).astype(BF16)
    v_aug = [jnp.concatenate([v_t[h * ATT_HD:(h + 1) * ATT_HD], ones_rows], axis=0) for h in range(2)]
    key_i = lax.broadcasted_iota(jnp.int32, (MOBA_BLOCK, MOBA_BLOCK), 0)
    query_i = lax.broadcasted_iota(jnp.int32, (MOBA_BLOCK, MOBA_BLOCK), 1)
    causal_t = key_i <= query_i

    def head_task(n, h, sink):
        r0 = n * MOBA_BLOCK
        width = r0 + MOBA_BLOCK
        s = lax.dot_general(ka[h][:width], qa[h][r0:width], _NT, preferred_element_type=F32)
        yield
        s_diag = jnp.where(causal_t, s[r0:], NEG)
        s = s_diag if n == 0 else jnp.concatenate([s[:r0], s_diag], axis=0)
        m = jnp.max(s, axis=0, keepdims=True)
        yield
        p = jnp.exp2(s - m).astype(BF16)
        yield
        o_t = jnp.dot(v_aug[h][:, :width], p, preferred_element_type=F32)
        sink[h] = o_t[:ATT_HD] / o_t[ATT_HD:ATT_HD + 1]
        if h == 1:
            o = jnp.concatenate([sink[0], sink[1]], axis=0).T
            g = g_ref[r0:width, :].astype(F32)
            o_ref[r0:width, :] = (o * _silu(g)).astype(BF16)

    sinks = [dict() for _ in range(nb)]
    _pipeline([head_task(n, h, sinks[n]) for n in range(nb) for h in range(2)])


def _moba(qkvg, kc, batch, t_len):
    m = batch * t_len
    blk = lambda kind: pl.BlockSpec((None, t_len, LANES), lambda b, hp: (kind * ATT_PAIRS + hp, b, 0))
    return pl.pallas_call(
        _moba_kernel,
        grid=(batch, ATT_PAIRS),
        in_specs=[blk(0), blk(1), blk(2), blk(3),
                  pl.BlockSpec((None, t_len, LANES), lambda b, hp: (hp, 0, 0))],
        out_specs=pl.BlockSpec((None, t_len, LANES), lambda b, hp: (hp, b, 0)),
        out_shape=jax.ShapeDtypeStruct((ATT_PAIRS, m, LANES), BF16),
        compiler_params=pltpu.CompilerParams(
            dimension_semantics=("arbitrary", "arbitrary"), vmem_limit_bytes=VMEM_LIMIT),
        name="moba",
    )(qkvg, qkvg, qkvg, qkvg, kc)


def _moba_key_table(t_len):
    pos = jnp.arange(t_len)
    blk = pos // MOBA_BLOCK
    off = pos % MOBA_BLOCK
    slopes = jnp.exp2(-8.0 * jnp.arange(1, ATT_HEADS + 1, dtype=F32) / ATT_HEADS)
    onehot = (blk[:, None] == jnp.arange(8)[None, :]).astype(F32)
    half = LANES // 2

    def head_table(h):
        in_blk = (slopes[h] * off.astype(F32))[:, None]
        blk_start = (slopes[h] * MOBA_BLOCK * blk.astype(F32))[:, None]
        cols = [onehot, in_blk, in_blk, in_blk, blk_start, blk_start, blk_start,
                jnp.zeros((t_len, half - 14), F32)]
        return jnp.concatenate(cols, axis=1)

    tables = [jnp.concatenate([head_table(2 * hp + 1), head_table(2 * hp)], axis=1)
              for hp in range(ATT_PAIRS)]
    return jnp.stack(tables).astype(BF16)


def _ssd_kernel(xs_ref, bm_ref, cm_ref, z_ref, dt_ref, shift_ref, cw_ref, cb_ref,
                dtb_ref, alog_ref, dskip_ref, nw_ref, o_ref):
    t_len = xs_ref.shape[1]
    nchunks = t_len // SSD_CHUNK
    conv_width = SSD_GW + 2 * SSD_STATE
    halo_rows = 8

    dtb = dtb_ref[...]
    hrow = lax.broadcasted_iota(jnp.int32, (LANES, SSD_GW), 0)
    lcol = lax.broadcasted_iota(jnp.int32, (LANES, SSD_GW), 1)
    li = lax.broadcasted_iota(jnp.int32, (SSD_CHUNK, SSD_CHUNK), 0)
    si = lax.broadcasted_iota(jnp.int32, (SSD_CHUNK, SSD_CHUNK), 1)
    tri = li >= si
    lmat = tri.astype(BF16)
    head_of_lane = lax.broadcasted_iota(jnp.int32, (SSD_CHUNK, SSD_GW), 1) // SSD_HD

    def stream_chunk(j, rows, h_t, halo):
        grp = pl.program_id(1) * SSD_STREAMS + j
        cw = cw_ref[j]
        a_e = -jnp.exp(alog_ref[j]) * LOG2E
        expand = (hrow == SSD_HPG * grp + lcol // SSD_HD).astype(BF16)

        dt = jnp.logaddexp(dt_ref[rows, :] + dtb, 0.0)
        dte = None
        for term in _split3(dt):
            part = jnp.dot(term, expand, preferred_element_type=F32)
            dte = part if dte is None else dte + part
        yield
        raw = jnp.concatenate([xs_ref[j, rows, :], bm_ref[j, rows, :], cm_ref[j, rows, :]], axis=1)
        shifted = jnp.dot(shift_ref[...], raw, preferred_element_type=F32)
        da = dte * a_e
        cs = None
        for term in _split3(da):
            part = jnp.dot(lmat, term, preferred_element_type=F32)
            cs = part if cs is None else cs + part
        z = z_ref[j, rows, :].astype(F32)
        zgate = _silu(z)
        yield
        cs_t = cs.T
        cs_last = cs[SSD_CHUNK - 1:SSD_CHUNK, :]
        decays = []
        for k in range(SSD_HPG):
            colb = jnp.broadcast_to(cs[:, SSD_HD * k:SSD_HD * k + 1], (SSD_CHUNK, SSD_CHUNK))
            rowb = cs_t[SSD_HD * k:SSD_HD * k + 1, :]
            decays.append(jnp.exp2(jnp.where(tri, colb - rowb, NEG)))
        to_end = jnp.exp2(cs_last - cs)
        from_start = jnp.exp2(cs)
        yield
        hwin = jnp.concatenate([halo, jnp.zeros_like(halo)], axis=0)
        acc = cb_ref[j]
        edge = None
        for s in range(SSD_CONV):
            tap = cw[SSD_CONV - 1 - s:SSD_CONV - s, :]
            acc = acc + shifted[s * SSD_CHUNK:(s + 1) * SSD_CHUNK] * tap
            if s:
                part = pltpu.roll(hwin, s, axis=0)[halo_rows:] * tap
                edge = part if edge is None else edge + part
        acc = jnp.concatenate([acc[:halo_rows] + edge, acc[halo_rows:]], axis=0)
        xbc = _silu(acc)
        xs = xbc[:, :SSD_GW]
        bm = xbc[:, SSD_GW:SSD_GW + SSD_STATE]
        cm = xbc[:, SSD_GW + SSD_STATE:]
        xdt = xs * dte
        xdt_b = xdt.astype(BF16)
        cm_b = cm.astype(BF16)
        cb = lax.dot_general(cm_b, bm.astype(BF16), _NT, preferred_element_type=F32)
        y_off = jnp.dot(cm_b, h_t.astype(BF16), preferred_element_type=F32)
        st = jnp.dot(bm.T.astype(BF16), (xdt * to_end).astype(BF16), preferred_element_type=F32)
        yield
        ws = [(cb * decays[k]).astype(BF16) for k in range(SSD_HPG)]
        xm = [jnp.where(head_of_lane == k, xdt_b, jnp.zeros_like(xdt_b)) for k in range(SSD_HPG)]
        y = jnp.dot(jnp.concatenate(ws, axis=1), jnp.concatenate(xm, axis=0),
                    preferred_element_type=F32)
        yield
        y = y + y_off * from_start + dskip_ref[j] * xs
        gated = y * zgate
        var = jnp.mean(gated * gated, axis=-1, keepdims=True)
        o_ref[j, rows, :] = (gated * lax.rsqrt(var + EPS) * nw_ref[j]).astype(BF16)
        return h_t * jnp.exp2(cs_last) + st, shifted[SSD_CHUNK - halo_rows:SSD_CHUNK, :]

    def chunk(c, carry):
        rows = pl.ds(pl.multiple_of(c * SSD_CHUNK, SSD_CHUNK), SSD_CHUNK)
        return _round_robin([stream_chunk(j, rows, *carry[j]) for j in range(SSD_STREAMS)])

    init = (jnp.zeros((SSD_STATE, SSD_GW), F32), jnp.zeros((halo_rows, conv_width), F32))
    lax.fori_loop(0, nchunks, chunk, (init,) * SSD_STREAMS)


def _ssd(xs, bm, cm, z, dt, conv_w, conv_b, dt_bias, a_log, d_skip, norm_w, batch, t_len):
    m = batch * t_len
    g_blk = lambda w: pl.BlockSpec((SSD_STREAMS, t_len, w), lambda b, g: (g, b, 0))
    g_par = lambda r, w: pl.BlockSpec((SSD_STREAMS, r, w), lambda b, g: (g, 0, 0))
    bc_w = SSD_GROUPS * SSD_STATE

    def per_group(a):
        r = a.shape[0]
        return jnp.concatenate([a[:, :SSD_W].reshape(r, SSD_GROUPS, SSD_GW),
                                a[:, SSD_W:SSD_W + bc_w].reshape(r, SSD_GROUPS, SSD_STATE),
                                a[:, SSD_W + bc_w:].reshape(r, SSD_GROUPS, SSD_STATE)], axis=2).transpose(1, 0, 2)

    dtb = jnp.pad(dt_bias, (0, LANES - SSD_HEADS)).reshape(1, LANES)
    per_lane = lambda a: jnp.repeat(a, SSD_HD).reshape(SSD_GROUPS, 1, SSD_GW)
    conv_width = SSD_GW + 2 * SSD_STATE
    t_out = jnp.arange(SSD_CONV * SSD_CHUNK)
    shift_table = (t_out[:, None] % SSD_CHUNK - t_out[:, None] // SSD_CHUNK
                   == jnp.arange(SSD_CHUNK)[None, :]).astype(BF16)
    return pl.pallas_call(
        _ssd_kernel,
        grid=(batch, SSD_GROUPS // SSD_STREAMS),
        in_specs=[g_blk(SSD_GW), g_blk(SSD_STATE), g_blk(SSD_STATE), g_blk(SSD_GW),
                  pl.BlockSpec((t_len, LANES), lambda b, g: (b, 0)),
                  pl.BlockSpec((SSD_CONV * SSD_CHUNK, SSD_CHUNK), lambda b, g: (0, 0)),
                  g_par(SSD_CONV, conv_width), g_par(1, conv_width),
                  pl.BlockSpec((1, LANES), lambda b, g: (0, 0)),
                  g_par(1, SSD_GW), g_par(1, SSD_GW), g_par(1, SSD_GW)],
        out_specs=g_blk(SSD_GW),
        out_shape=jax.ShapeDtypeStruct((SSD_GROUPS, m, SSD_GW), BF16),
        compiler_params=pltpu.CompilerParams(
            dimension_semantics=("arbitrary", "arbitrary"), vmem_limit_bytes=VMEM_LIMIT),
        name="ssd",
    )(xs, bm, cm, z, dt, shift_table, per_group(conv_w), per_group(conv_b.reshape(1, SSD_XBC)), dtb,
      per_lane(a_log), per_lane(d_skip), norm_w.reshape(SSD_GROUPS, 1, SSD_GW))


def _mem_kernel(mem_ref, mnw_ref, wkv_ref, q_ref, g_ref, o_ref, *, q_chunk):
    t_len = q_ref.shape[1]
    mem = mem_ref[...]
    ms = jnp.mean(mem * mem, axis=-1, keepdims=True)
    mem_n = (mem * lax.rsqrt(ms + EPS) * mnw_ref[...]).astype(BF16)
    kv = jnp.dot(mem_n, wkv_ref[...], preferred_element_type=F32).astype(BF16)
    scale = MEM_HD ** -0.5
    for h in range(MEM_HEADS):
        k = kv[:, h * MEM_HD:(h + 1) * MEM_HD]
        v = kv[:, MEM_W + h * MEM_HD:MEM_W + (h + 1) * MEM_HD]
        for r0 in range(0, t_len, q_chunk):
            s = lax.dot_general(q_ref[h, r0:r0 + q_chunk, :], k, _NT, preferred_element_type=F32) * scale
            m = jnp.max(s, axis=-1, keepdims=True)
            p = jnp.exp(s - m)
            l = jnp.sum(p, axis=-1, keepdims=True)
            o = jnp.dot(p.astype(BF16), v, preferred_element_type=F32) / l
            g = g_ref[h, r0:r0 + q_chunk, :].astype(F32)
            o_ref[h, r0:r0 + q_chunk, :] = (o * _silu(g)).astype(BF16)


def _mem_attn(mem2d, mem_norm_w, w_kv, qm, gm, batch, t_len, mem_len):
    m = batch * t_len
    return pl.pallas_call(
        functools.partial(_mem_kernel, q_chunk=512),
        grid=(batch,),
        in_specs=[pl.BlockSpec((mem_len, D_MODEL), lambda b: (b, 0)),
                  pl.BlockSpec((1, D_MODEL), lambda b: (0, 0)),
                  pl.BlockSpec((D_MODEL, 2 * MEM_W), lambda b: (0, 0), pipeline_mode=pl.Buffered(1)),
                  pl.BlockSpec((MEM_HEADS, t_len, MEM_HD), lambda b: (0, b, 0)),
                  pl.BlockSpec((MEM_HEADS, t_len, MEM_HD), lambda b: (0, b, 0))],
        out_specs=pl.BlockSpec((MEM_HEADS, t_len, MEM_HD), lambda b: (0, b, 0)),
        out_shape=jax.ShapeDtypeStruct((MEM_HEADS, m, MEM_HD), BF16),
        compiler_params=pltpu.CompilerParams(
            dimension_semantics=("arbitrary",), vmem_limit_bytes=VMEM_LIMIT),
        name="mem_attn",
    )(mem2d, mem_norm_w, w_kv, qm, gm)


def _outproj_kernel(oatt_ref, ossd_ref, omem_ref, x_ref, w_ref, fnw_ref, o_ref):
    pieces = ([oatt_ref[i] for i in range(ATT_PAIRS)] + [ossd_ref[i] for i in range(SSD_GROUPS)]
              + [omem_ref[i] for i in range(MEM_HEADS)])
    mixed = jnp.concatenate(pieces, axis=-1)
    h = x_ref[...] + jnp.dot(mixed, w_ref[...], preferred_element_type=F32)
    ms = jnp.mean(h * h, axis=-1, keepdims=True)
    o_ref[...] = h * lax.rsqrt(ms + EPS) * fnw_ref[...]


def _out_proj(oatt, ossd, omem, x2d, w_out, final_norm_w, tm):
    m = x2d.shape[0]
    lead = lambda i: (0, i, 0)
    return pl.pallas_call(
        _outproj_kernel,
        grid=(m // tm,),
        in_specs=[pl.BlockSpec((ATT_PAIRS, tm, LANES), lead),
                  pl.BlockSpec((SSD_GROUPS, tm, SSD_GW), lead),
                  pl.BlockSpec((MEM_HEADS, tm, MEM_HD), lead),
                  pl.BlockSpec((tm, D_MODEL), lambda i: (i, 0)),
                  pl.BlockSpec((MIX_W, D_MODEL), lambda i: (0, 0), pipeline_mode=pl.Buffered(1)),
                  pl.BlockSpec((1, D_MODEL), lambda i: (0, 0))],
        out_specs=pl.BlockSpec((tm, D_MODEL), lambda i: (i, 0)),
        out_shape=jax.ShapeDtypeStruct((m, D_MODEL), F32),
        compiler_params=pltpu.CompilerParams(
            dimension_semantics=("arbitrary",), vmem_limit_bytes=VMEM_LIMIT),
        name="out_proj",
    )(oatt, ossd, omem, x2d, w_out, final_norm_w)


def kernel(x, mem, norm_w, w_in, conv_w, conv_b, dt_bias, a_log, d_skip, ssd_norm_w,
           mem_norm_w, w_mem_kv, w_out, final_norm_w):
    batch, t_len, d_model = x.shape
    mem_len = mem.shape[1]
    depth = norm_w.shape[0]
    assert d_model == D_MODEL and depth == 1 and t_len % MOBA_BLOCK == 0 and t_len // MOBA_BLOCK == 8
    dt0 = 4 * ATT_W + SSD_W + SSD_XBC
    x2d = x.reshape(batch * t_len, D_MODEL)
    mem2d = mem.reshape(batch * mem_len, D_MODEL)
    w = w_in[0]
    w_main = jnp.concatenate([w[:, :dt0], w[:, dt0 + SSD_HEADS:]], axis=1).astype(BF16)
    w_dt = jnp.pad(w[:, dt0:dt0 + SSD_HEADS], ((0, 0), (0, LANES - SSD_HEADS))).astype(BF16)

    qkvg, z, xs, bm, cm, qm, gm, dt = _in_proj(x2d, norm_w[0].reshape(1, D_MODEL), w_main, w_dt, tm=512)
    o_att = _moba(qkvg, _moba_key_table(t_len), batch, t_len)
    o_ssd = _ssd(xs, bm, cm, z, dt, conv_w[0], conv_b[0], dt_bias[0], a_log[0], d_skip[0],
                 ssd_norm_w[0], batch, t_len)
    o_mem = _mem_attn(mem2d, mem_norm_w[0].reshape(1, D_MODEL), w_mem_kv[0].astype(BF16), qm, gm,
                      batch, t_len, mem_len)
    out = _out_proj(o_att, o_ssd, o_mem, x2d, w_out[0].astype(BF16), final_norm_w.reshape(1, D_MODEL), tm=512)
    return out.reshape(batch, t_len, D_MODEL)
```

```python
import functools

import jax
import jax.numpy as jnp
import numpy as np
from jax import lax
from jax.experimental import pallas as pl
from jax.experimental.pallas import tpu as pltpu

F32 = jnp.float32
BF16 = jnp.bfloat16

D_MODEL = 1024
ATT_HEADS = 8
ATT_HD = 64
ATT_W = ATT_HEADS * ATT_HD
ATT_PAIRS = ATT_HEADS // 2
MOBA_BLOCK = 256
MOBA_TOPK = 3
SSD_W = 1024
SSD_HD = 64
SSD_HEADS = 16
SSD_GROUPS = 4
SSD_HPG = SSD_HEADS // SSD_GROUPS
SSD_GW = SSD_W // SSD_GROUPS
SSD_STREAMS = 2
SSD_STATE = 128
SSD_CONV = 4
SSD_CHUNK = 256
SSD_XBC = SSD_W + 2 * SSD_GROUPS * SSD_STATE
MEM_HEADS = 4
MEM_HD = 128
MEM_W = MEM_HEADS * MEM_HD
MIX_W = ATT_W + SSD_W + MEM_W
EPS = 1e-6
NEG = -1e30
LOG2E = 1.4426950408889634
_LOG2E_TERMS = (1.4453125, -0.00262451171875, 7.063150405883789e-06)
LANES = 128
PROJ_TILE = 1024
PROJ_SUB = 256
MAIN_W = 4 * ATT_W + SSD_W + SSD_XBC + 2 * MEM_W
VMEM_LIMIT = 56 * 1024 * 1024

_NT = (((1,), (1,)), ((), ()))


def _split2(a):
    hi = a.astype(BF16)
    lo = (a - hi.astype(F32)).astype(BF16)
    return hi, lo


def _silu(a):
    return a * jax.nn.sigmoid(a)


def _pipeline(gens):
    live = []
    pending = list(gens)
    while pending or live:
        if pending:
            live.append(pending.pop(0))
        for g in list(reversed(live)):
            try:
                next(g)
            except StopIteration:
                live.remove(g)


def _round_robin(gens):
    results = [None] * len(gens)
    active = list(range(len(gens)))
    while active:
        for j in list(active):
            try:
                next(gens[j])
            except StopIteration as done:
                results[j] = done.value
                active.remove(j)
    return tuple(results)


def _inproj_kernel(x_ref, nw_ref, w_ref, wdt_ref,
                   qkvg_ref, z_ref, xs_ref, bm_ref, cm_ref, qm_ref, gm_ref, dt_ref):
    tm = x_ref.shape[0]
    n_sub = tm // PROJ_SUB
    nw = nw_ref[...]

    def normed(sb):
        x = x_ref[sb * PROJ_SUB:(sb + 1) * PROJ_SUB, :]
        ms = jnp.mean(x * x, axis=-1, keepdims=True)
        return (x * lax.rsqrt(ms + EPS) * nw).astype(BF16)

    u_next = normed(0)
    for sb in range(n_sub):
        u = u_next
        if sb + 1 < n_sub:
            u_next = normed(sb + 1)
        rows = slice(sb * PROJ_SUB, (sb + 1) * PROJ_SUB)

        def scatter(ref, c0, piece, count):
            for c in range(0, piece * count, 512):
                r = jnp.dot(u, w_ref[:, c0 + c:c0 + c + 512], preferred_element_type=F32).astype(BF16)
                for j in range(512 // piece):
                    ref[(c + j * piece) // piece, rows, :] = r[:, j * piece:(j + 1) * piece]

        scatter(qkvg_ref, 0, LANES, 16)
        scatter(z_ref, 4 * ATT_W, SSD_GW, SSD_GROUPS)
        scatter(xs_ref, 4 * ATT_W + SSD_W, SSD_GW, SSD_GROUPS)
        scatter(bm_ref, 4 * ATT_W + 2 * SSD_W, SSD_STATE, SSD_GROUPS)
        scatter(cm_ref, 4 * ATT_W + 2 * SSD_W + SSD_GROUPS * SSD_STATE, SSD_STATE, SSD_GROUPS)
        scatter(qm_ref, 4 * ATT_W + SSD_W + SSD_XBC, MEM_HD, MEM_HEADS)
        scatter(gm_ref, 4 * ATT_W + SSD_W + SSD_XBC + MEM_W, MEM_HD, MEM_HEADS)
        dt_ref[rows, :] = jnp.dot(u, wdt_ref[...], preferred_element_type=F32)


def _in_proj(x2d, norm_w, w_main, w_dt, tm):
    m = x2d.shape[0]
    const = lambda i: (0, 0)
    lead = lambda i: (0, i, 0)
    out_shape = (
        jax.ShapeDtypeStruct((16, m, LANES), BF16),
        jax.ShapeDtypeStruct((SSD_GROUPS, m, SSD_GW), BF16),
        jax.ShapeDtypeStruct((SSD_GROUPS, m, SSD_GW), BF16),
        jax.ShapeDtypeStruct((SSD_GROUPS, m, SSD_STATE), BF16),
        jax.ShapeDtypeStruct((SSD_GROUPS, m, SSD_STATE), BF16),
        jax.ShapeDtypeStruct((MEM_HEADS, m, MEM_HD), BF16),
        jax.ShapeDtypeStruct((MEM_HEADS, m, MEM_HD), BF16),
        jax.ShapeDtypeStruct((m, LANES), F32),
    )
    out_specs = (
        pl.BlockSpec((16, tm, LANES), lead),
        pl.BlockSpec((SSD_GROUPS, tm, SSD_GW), lead),
        pl.BlockSpec((SSD_GROUPS, tm, SSD_GW), lead),
        pl.BlockSpec((SSD_GROUPS, tm, SSD_STATE), lead),
        pl.BlockSpec((SSD_GROUPS, tm, SSD_STATE), lead),
        pl.BlockSpec((MEM_HEADS, tm, MEM_HD), lead),
        pl.BlockSpec((MEM_HEADS, tm, MEM_HD), lead),
        pl.BlockSpec((tm, LANES), lambda i: (i, 0)),
    )
    return pl.pallas_call(
        _inproj_kernel,
        grid=(m // tm,),
        in_specs=[
            pl.BlockSpec((tm, D_MODEL), lambda i: (i, 0)),
            pl.BlockSpec((1, D_MODEL), const),
            pl.BlockSpec((D_MODEL, MAIN_W), const, pipeline_mode=pl.Buffered(1)),
            pl.BlockSpec((D_MODEL, LANES), const, pipeline_mode=pl.Buffered(1)),
        ],
        out_specs=out_specs,
        out_shape=out_shape,
        compiler_params=pltpu.CompilerParams(
            dimension_semantics=("arbitrary",), vmem_limit_bytes=VMEM_LIMIT),
        name="in_proj",
    )(x2d, norm_w, w_main, w_dt)


def _moba_kernel(q_ref, k_ref, v_ref, g_ref, kc_ref, o_ref):
    t_len = q_ref.shape[0]
    nb = t_len // MOBA_BLOCK
    half = LANES // 2

    q = q_ref[...]
    k = k_ref[...]
    v = v_ref[...]
    kc = kc_ref[...]
    lo = lax.broadcasted_iota(jnp.int32, (t_len, LANES), 1) < half

    kmean = jnp.mean(k.astype(F32).reshape(nb, MOBA_BLOCK, LANES), axis=1)
    lo8 = lax.broadcasted_iota(jnp.int32, (nb, LANES), 1) < half
    terms = []
    for km in (jnp.where(lo8, kmean, 0.0), jnp.where(lo8, 0.0, kmean)):
        hi = km.astype(BF16).astype(F32)
        terms += [hi, km - hi]
    kstack = jnp.concatenate(terms, axis=0).astype(BF16)
    gate_t = lax.dot_general(kstack, q, _NT, preferred_element_type=F32)

    tpos = lax.broadcasted_iota(jnp.int32, (nb, t_len), 1)
    qblk = tpos // MOBA_BLOCK
    jidx = lax.broadcasted_iota(jnp.int32, (nb, t_len), 0)

    def select_bias(gate):
        cnt = jnp.zeros((nb, t_len), jnp.int32)
        for i in range(nb):
            gi = gate[i:i + 1, :]
            beats = (gi > gate) | ((gi == gate) & (i < jidx))
            cnt = cnt + jnp.where(beats & (i < qblk), 1, 0)
        sel = ((cnt < MOBA_TOPK) & (jidx < qblk)) | (jidx == qblk)
        return jnp.where(sel, 0.0, NEG)

    bias0 = select_bias(gate_t[0:nb] + gate_t[nb:2 * nb])
    bias1 = select_bias(gate_t[2 * nb:3 * nb] + gate_t[3 * nb:4 * nb])
    crow = lax.broadcasted_iota(jnp.int32, (8, t_len), 0)
    l2e = jnp.zeros((8, t_len), F32)
    for i, term in enumerate(_LOG2E_TERMS + _LOG2E_TERMS):
        l2e = jnp.where(crow == i, term, l2e)
    pad = jnp.zeros((half - nb - 8, t_len), F32)
    aug = jnp.concatenate([bias1, l2e, pad, bias0, l2e, pad], axis=0).T.astype(BF16)

    qs = (q.astype(F32) * (ATT_HD ** -0.5 * LOG2E)).astype(BF16)
    qa = (jnp.where(lo, qs, aug), jnp.where(lo, aug, qs))
    ka = (jnp.where(lo, k, kc), jnp.where(lo, kc, k))

    v_t = v.astype(F32).T.astype(BF16)
    ones_rows = (lax.broadcasted_iota(jnp.int32, (16, t_len), 0) == 0).astype(BF16)
    v_aug = [jnp.concatenate([v_t[h * ATT_HD:(h + 1) * ATT_HD], ones_rows], axis=0) for h in range(2)]
    key_i = lax.broadcasted_iota(jnp.int32, (MOBA_BLOCK, MOBA_BLOCK), 0)
    query_i = lax.broadcasted_iota(jnp.int32, (MOBA_BLOCK, MOBA_BLOCK), 1)
    causal_t = key_i <= query_i

    def head_task(n, h, sink):
        r0 = n * MOBA_BLOCK
        width = r0 + MOBA_BLOCK
        s = lax.dot_general(ka[h][:width], qa[h][r0:width], _NT, preferred_element_type=F32)
        yield
        s_diag = jnp.where(causal_t, s[r0:], NEG)
        s = s_diag if n == 0 else jnp.concatenate([s[:r0], s_diag], axis=0)
        m = jnp.max(s, axis=0, keepdims=True)
        yield
        p = jnp.exp2(s - m).astype(BF16)
        yield
        o_t = jnp.dot(v_aug[h][:, :width], p, preferred_element_type=F32)
        sink[h] = o_t[:ATT_HD] / o_t[ATT_HD:ATT_HD + 1]
        if h == 1:
            o = jnp.concatenate([sink[0], sink[1]], axis=0).T
            g = g_ref[r0:width, :].astype(F32)
            o_ref[r0:width, :] = (o * _silu(g)).astype(BF16)

    sinks = [dict() for _ in range(nb)]
    _pipeline([head_task(n, h, sinks[n]) for n in range(nb) for h in range(2)])


def _moba(qkvg, kc, batch, t_len):
    m = batch * t_len
    blk = lambda kind: pl.BlockSpec((None, t_len, LANES), lambda b, hp: (kind * ATT_PAIRS + hp, b, 0))
    return pl.pallas_call(
        _moba_kernel,
        grid=(batch, ATT_PAIRS),
        in_specs=[blk(0), blk(1), blk(2), blk(3),
                  pl.BlockSpec((None, t_len, LANES), lambda b, hp: (hp, 0, 0))],
        out_specs=pl.BlockSpec((None, t_len, LANES), lambda b, hp: (hp, b, 0)),
        out_shape=jax.ShapeDtypeStruct((ATT_PAIRS, m, LANES), BF16),
        compiler_params=pltpu.CompilerParams(
            dimension_semantics=("arbitrary", "arbitrary"), vmem_limit_bytes=VMEM_LIMIT),
        name="moba",
    )(qkvg, qkvg, qkvg, qkvg, kc)


def _moba_key_table(t_len):
    pos = np.arange(t_len)
    blk = pos // MOBA_BLOCK
    off = pos % MOBA_BLOCK
    slopes = np.exp2(-8.0 * np.arange(1, ATT_HEADS + 1, dtype=np.float32) / ATT_HEADS)
    onehot = (blk[:, None] == np.arange(8)[None, :]).astype(np.float32)
    half = LANES // 2

    def head_table(h):
        in_blk = (slopes[h] * off.astype(np.float32))[:, None]
        blk_start = (slopes[h] * MOBA_BLOCK * blk.astype(np.float32))[:, None]
        cols = [onehot, in_blk, in_blk, in_blk, blk_start, blk_start, blk_start,
                np.zeros((t_len, half - 14), np.float32)]
        return np.concatenate(cols, axis=1)

    tables = [np.concatenate([head_table(2 * hp + 1), head_table(2 * hp)], axis=1)
              for hp in range(ATT_PAIRS)]
    return jnp.asarray(np.stack(tables), BF16)


def _ssd_kernel(xs_ref, bm_ref, cm_ref, z_ref, dt_ref, shift_ref, cw_ref, cb_ref,
                dtb_ref, alog_ref, dskip_ref, nw_ref, o_ref):
    t_len = xs_ref.shape[1]
    nchunks = t_len // SSD_CHUNK
    conv_width = SSD_GW + 2 * SSD_STATE
    halo_rows = 8

    dtb = dtb_ref[...]
    hrow = lax.broadcasted_iota(jnp.int32, (LANES, SSD_GW), 0)
    lcol = lax.broadcasted_iota(jnp.int32, (LANES, SSD_GW), 1)
    li = lax.broadcasted_iota(jnp.int32, (SSD_CHUNK, SSD_CHUNK), 0)
    si = lax.broadcasted_iota(jnp.int32, (SSD_CHUNK, SSD_CHUNK), 1)
    tri = li >= si
    lmat = tri.astype(BF16)
    head_of_lane = lax.broadcasted_iota(jnp.int32, (SSD_CHUNK, SSD_GW), 1) // SSD_HD

    a_es = [-jnp.exp(alog_ref[j]) * LOG2E for j in range(SSD_STREAMS)]
    expands = [(hrow == SSD_HPG * (pl.program_id(1) * SSD_STREAMS + j) + lcol // SSD_HD).astype(BF16)
               for j in range(SSD_STREAMS)]

    def stream_chunk(j, rows, h_t, halo):
        cw = cw_ref[j]
        a_e, expand = a_es[j], expands[j]

        dt = jnp.logaddexp(dt_ref[rows, :] + dtb, 0.0)
        dte = None
        for term in _split2(dt):
            part = jnp.dot(term, expand, preferred_element_type=F32)
            dte = part if dte is None else dte + part
        yield
        raw = jnp.concatenate([xs_ref[j, rows, :], bm_ref[j, rows, :], cm_ref[j, rows, :]], axis=1)
        shifted = jnp.dot(shift_ref[...], raw, preferred_element_type=F32)
        raw_f = raw.astype(F32)
        da = dte * a_e
        cs = None
        for term in _split2(da):
            part = jnp.dot(lmat, term, preferred_element_type=F32)
            cs = part if cs is None else cs + part
        z = z_ref[j, rows, :].astype(F32)
        zgate = _silu(z)
        yield
        cs_t = cs.T
        cs_last = cs[SSD_CHUNK - 1:SSD_CHUNK, :]
        decays = []
        for k in range(SSD_HPG):
            colb = jnp.broadcast_to(cs[:, SSD_HD * k:SSD_HD * k + 1], (SSD_CHUNK, SSD_CHUNK))
            rowb = cs_t[SSD_HD * k:SSD_HD * k + 1, :]
            decays.append(jnp.exp2(jnp.where(tri, colb - rowb, NEG)))
        to_end = jnp.exp2(cs_last - cs)
        from_start = jnp.exp2(cs)
        yield
        hwin = jnp.concatenate([halo, jnp.zeros_like(halo)], axis=0)
        acc = cb_ref[j] + raw_f * cw[SSD_CONV - 1:SSD_CONV, :]
        edge = None
        for s in range(1, SSD_CONV):
            tap = cw[SSD_CONV - 1 - s:SSD_CONV - s, :]
            acc = acc + shifted[(s - 1) * SSD_CHUNK:s * SSD_CHUNK] * tap
            part = pltpu.roll(hwin, s, axis=0)[halo_rows:] * tap
            edge = part if edge is None else edge + part
        acc = jnp.concatenate([acc[:halo_rows] + edge, acc[halo_rows:]], axis=0)
        xbc = _silu(acc)
        xs = xbc[:, :SSD_GW]
        bm = xbc[:, SSD_GW:SSD_GW + SSD_STATE]
        cm = xbc[:, SSD_GW + SSD_STATE:]
        xdt = xs * dte
        xdt_b = xdt.astype(BF16)
        cm_b = cm.astype(BF16)
        cb = lax.dot_general(cm_b, bm.astype(BF16), _NT, preferred_element_type=F32)
        y_off = jnp.dot(cm_b, h_t.astype(BF16), preferred_element_type=F32)
        st = jnp.dot(bm.T.astype(BF16), (xdt * to_end).astype(BF16), preferred_element_type=F32)
        yield
        ws = [(cb * decays[k]).astype(BF16) for k in range(SSD_HPG)]
        xm = [jnp.where(head_of_lane == k, xdt_b, jnp.zeros_like(xdt_b)) for k in range(SSD_HPG)]
        y = jnp.dot(jnp.concatenate(ws, axis=1), jnp.concatenate(xm, axis=0),
                    preferred_element_type=F32)
        yield
        y = y + y_off * from_start + dskip_ref[j] * xs
        gated = y * zgate
        var = jnp.mean(gated * gated, axis=-1, keepdims=True)
        o_ref[j, rows, :] = (gated * lax.rsqrt(var + EPS) * nw_ref[j]).astype(BF16)
        return h_t * jnp.exp2(cs_last) + st, raw_f[SSD_CHUNK - halo_rows:, :]

    def chunk(c, carry):
        rows = pl.ds(pl.multiple_of(c * SSD_CHUNK, SSD_CHUNK), SSD_CHUNK)
        return _round_robin([stream_chunk(j, rows, *carry[j]) for j in range(SSD_STREAMS)])

    init = (jnp.zeros((SSD_STATE, SSD_GW), F32), jnp.zeros((halo_rows, conv_width), F32))
    lax.fori_loop(0, nchunks, chunk, (init,) * SSD_STREAMS)


def _ssd(xs, bm, cm, z, dt, conv_w, conv_b, dt_bias, a_log, d_skip, norm_w, batch, t_len):
    m = batch * t_len
    g_blk = lambda w: pl.BlockSpec((SSD_STREAMS, t_len, w), lambda b, g: (g, b, 0))
    g_par = lambda r, w: pl.BlockSpec((SSD_STREAMS, r, w), lambda b, g: (g, 0, 0))
    bc_w = SSD_GROUPS * SSD_STATE

    def per_group(a):
        r = a.shape[0]
        return jnp.concatenate([a[:, :SSD_W].reshape(r, SSD_GROUPS, SSD_GW),
                                a[:, SSD_W:SSD_W + bc_w].reshape(r, SSD_GROUPS, SSD_STATE),
                                a[:, SSD_W + bc_w:].reshape(r, SSD_GROUPS, SSD_STATE)], axis=2).transpose(1, 0, 2)

    dtb = jnp.pad(dt_bias, (0, LANES - SSD_HEADS)).reshape(1, LANES)
    per_lane = lambda a: jnp.repeat(a, SSD_HD).reshape(SSD_GROUPS, 1, SSD_GW)
    conv_width = SSD_GW + 2 * SSD_STATE
    t_out = np.arange((SSD_CONV - 1) * SSD_CHUNK)[:, None]
    shift_table = jnp.asarray(t_out % SSD_CHUNK - (t_out // SSD_CHUNK + 1) == np.arange(SSD_CHUNK)[None, :], BF16)
    return pl.pallas_call(
        _ssd_kernel,
        grid=(batch, SSD_GROUPS // SSD_STREAMS),
        in_specs=[g_blk(SSD_GW), g_blk(SSD_STATE), g_blk(SSD_STATE), g_blk(SSD_GW),
                  pl.BlockSpec((t_len, LANES), lambda b, g: (b, 0)),
                  pl.BlockSpec(((SSD_CONV - 1) * SSD_CHUNK, SSD_CHUNK), lambda b, g: (0, 0)),
                  g_par(SSD_CONV, conv_width), g_par(1, conv_width),
                  pl.BlockSpec((1, LANES), lambda b, g: (0, 0)),
                  g_par(1, SSD_GW), g_par(1, SSD_GW), g_par(1, SSD_GW)],
        out_specs=g_blk(SSD_GW),
        out_shape=jax.ShapeDtypeStruct((SSD_GROUPS, m, SSD_GW), BF16),
        compiler_params=pltpu.CompilerParams(
            dimension_semantics=("arbitrary", "arbitrary"), vmem_limit_bytes=VMEM_LIMIT),
        name="ssd",
    )(xs, bm, cm, z, dt, shift_table, per_group(conv_w), per_group(conv_b.reshape(1, SSD_XBC)), dtb,
      per_lane(a_log), per_lane(d_skip), norm_w.reshape(SSD_GROUPS, 1, SSD_GW))


def _mem_kernel(mem_ref, mnw_ref, wkv_ref, q_ref, g_ref, o_ref, *, q_chunk):
    t_len = q_ref.shape[1]
    mem = mem_ref[...]
    ms = jnp.mean(mem * mem, axis=-1, keepdims=True)
    mem_n = (mem * lax.rsqrt(ms + EPS) * mnw_ref[...]).astype(BF16)
    kv = jnp.dot(mem_n, wkv_ref[...], preferred_element_type=F32).astype(BF16)
    scale = MEM_HD ** -0.5
    for h in range(MEM_HEADS):
        k = kv[:, h * MEM_HD:(h + 1) * MEM_HD]
        v = kv[:, MEM_W + h * MEM_HD:MEM_W + (h + 1) * MEM_HD]
        for r0 in range(0, t_len, q_chunk):
            s = lax.dot_general(q_ref[h, r0:r0 + q_chunk, :], k, _NT, preferred_element_type=F32) * scale
            m = jnp.max(s, axis=-1, keepdims=True)
            p = jnp.exp(s - m)
            l = jnp.sum(p, axis=-1, keepdims=True)
            o = jnp.dot(p.astype(BF16), v, preferred_element_type=F32) / l
            g = g_ref[h, r0:r0 + q_chunk, :].astype(F32)
            o_ref[h, r0:r0 + q_chunk, :] = (o * _silu(g)).astype(BF16)


def _mem_attn(mem2d, mem_norm_w, w_kv, qm, gm, batch, t_len, mem_len):
    m = batch * t_len
    return pl.pallas_call(
        functools.partial(_mem_kernel, q_chunk=512),
        grid=(batch,),
        in_specs=[pl.BlockSpec((mem_len, D_MODEL), lambda b: (b, 0)),
                  pl.BlockSpec((1, D_MODEL), lambda b: (0, 0)),
                  pl.BlockSpec((D_MODEL, 2 * MEM_W), lambda b: (0, 0), pipeline_mode=pl.Buffered(1)),
                  pl.BlockSpec((MEM_HEADS, t_len, MEM_HD), lambda b: (0, b, 0)),
                  pl.BlockSpec((MEM_HEADS, t_len, MEM_HD), lambda b: (0, b, 0))],
        out_specs=pl.BlockSpec((MEM_HEADS, t_len, MEM_HD), lambda b: (0, b, 0)),
        out_shape=jax.ShapeDtypeStruct((MEM_HEADS, m, MEM_HD), BF16),
        compiler_params=pltpu.CompilerParams(
            dimension_semantics=("arbitrary",), vmem_limit_bytes=VMEM_LIMIT),
        name="mem_attn",
    )(mem2d, mem_norm_w, w_kv, qm, gm)


def _outproj_kernel(oatt_ref, ossd_ref, omem_ref, x_ref, w_ref, fnw_ref, o_ref):
    tm = x_ref.shape[0]
    n_sub = tm // PROJ_SUB
    fnw = fnw_ref[...]
    refs = ([(oatt_ref, i) for i in range(ATT_PAIRS)] + [(ossd_ref, i) for i in range(SSD_GROUPS)]
            + [(omem_ref, i) for i in range(MEM_HEADS)])
    h_prev = None
    for sb in range(n_sub + 1):
        if sb < n_sub:
            rows = slice(sb * PROJ_SUB, (sb + 1) * PROJ_SUB)
            mixed = jnp.concatenate([ref[i, rows, :] for ref, i in refs], axis=-1)
            h = x_ref[rows, :] + jnp.dot(mixed, w_ref[...], preferred_element_type=F32)
        if h_prev is not None:
            ms = jnp.mean(h_prev * h_prev, axis=-1, keepdims=True)
            o_ref[(sb - 1) * PROJ_SUB:sb * PROJ_SUB, :] = h_prev * lax.rsqrt(ms + EPS) * fnw
        h_prev = h


def _out_proj(oatt, ossd, omem, x2d, w_out, final_norm_w, tm):
    m = x2d.shape[0]
    lead = lambda i: (0, i, 0)
    return pl.pallas_call(
        _outproj_kernel,
        grid=(m // tm,),
        in_specs=[pl.BlockSpec((ATT_PAIRS, tm, LANES), lead),
                  pl.BlockSpec((SSD_GROUPS, tm, SSD_GW), lead),
                  pl.BlockSpec((MEM_HEADS, tm, MEM_HD), lead),
                  pl.BlockSpec((tm, D_MODEL), lambda i: (i, 0)),
                  pl.BlockSpec((MIX_W, D_MODEL), lambda i: (0, 0), pipeline_mode=pl.Buffered(1)),
                  pl.BlockSpec((1, D_MODEL), lambda i: (0, 0))],
        out_specs=pl.BlockSpec((tm, D_MODEL), lambda i: (i, 0)),
        out_shape=jax.ShapeDtypeStruct((m, D_MODEL), F32),
        compiler_params=pltpu.CompilerParams(
            dimension_semantics=("arbitrary",), vmem_limit_bytes=VMEM_LIMIT),
        name="out_proj",
    )(oatt, ossd, omem, x2d, w_out, final_norm_w)


def kernel(x, mem, norm_w, w_in, conv_w, conv_b, dt_bias, a_log, d_skip, ssd_norm_w,
           mem_norm_w, w_mem_kv, w_out, final_norm_w):
    batch, t_len, d_model = x.shape
    mem_len = mem.shape[1]
    depth = norm_w.shape[0]
    assert d_model == D_MODEL and depth == 1 and t_len % MOBA_BLOCK == 0 and t_len // MOBA_BLOCK == 8
    dt0 = 4 * ATT_W + SSD_W + SSD_XBC
    x2d = x.reshape(batch * t_len, D_MODEL)
    mem2d = mem.reshape(batch * mem_len, D_MODEL)
    w = w_in[0]
    w_main = jnp.concatenate([w[:, :dt0], w[:, dt0 + SSD_HEADS:]], axis=1).astype(BF16)
    w_dt = jnp.pad(w[:, dt0:dt0 + SSD_HEADS], ((0, 0), (0, LANES - SSD_HEADS))).astype(BF16)

    qkvg, z, xs, bm, cm, qm, gm, dt = _in_proj(x2d, norm_w[0].reshape(1, D_MODEL), w_main, w_dt, tm=PROJ_TILE)
    o_att = _moba(qkvg, _moba_key_table(t_len), batch, t_len)
    o_ssd = _ssd(xs, bm, cm, z, dt, conv_w[0], conv_b[0], dt_bias[0], a_log[0], d_skip[0],
                 ssd_norm_w[0], batch, t_len)
    o_mem = _mem_attn(mem2d, mem_norm_w[0].reshape(1, D_MODEL), w_mem_kv[0].astype(BF16), qm, gm,
                      batch, t_len, mem_len)
    out = _out_proj(o_att, o_ssd, o_mem, x2d, w_out[0].astype(BF16), final_norm_w.reshape(1, D_MODEL), tm=PROJ_TILE)
    return out.reshape(batch, t_len, D_MODEL)
```

```python
import functools

import jax
import jax.numpy as jnp
import numpy as np
from jax import lax
from jax.experimental import pallas as pl
from jax.experimental.pallas import tpu as pltpu

F32 = jnp.float32
BF16 = jnp.bfloat16

D_MODEL = 1024
ATT_HEADS = 8
ATT_HD = 64
ATT_W = ATT_HEADS * ATT_HD
ATT_PAIRS = ATT_HEADS // 2
MOBA_BLOCK = 256
MOBA_TOPK = 3
SSD_W = 1024
SSD_HD = 64
SSD_HEADS = 16
SSD_GROUPS = 4
SSD_HPG = SSD_HEADS // SSD_GROUPS
SSD_GW = SSD_W // SSD_GROUPS
SSD_STREAMS = 4
SSD_STATE = 128
SSD_CONV = 4
SSD_CHUNK = 256
SSD_XBC = SSD_W + 2 * SSD_GROUPS * SSD_STATE
MEM_HEADS = 4
MEM_HD = 128
MEM_W = MEM_HEADS * MEM_HD
MIX_W = ATT_W + SSD_W + MEM_W
EPS = 1e-6
NEG = -1e30
LOG2E = 1.4426950408889634
_LOG2E_TERMS = (1.4453125, -0.00262451171875, 7.063150405883789e-06)
LANES = 128
PROJ_TILE = 1024
PROJ_SUB = 256
DT_COL0 = 4 * ATT_W + SSD_W + SSD_XBC
VMEM_LIMIT = 56 * 1024 * 1024

_NT = (((1,), (1,)), ((), ()))


def _split2(a):
    hi = a.astype(BF16)
    lo = (a - hi.astype(F32)).astype(BF16)
    return hi, lo


def _silu(a):
    return a * jax.nn.sigmoid(a)


def _pipeline(gens):
    live = []
    pending = list(gens)
    while pending or live:
        if pending:
            live.append(pending.pop(0))
        for g in list(reversed(live)):
            try:
                next(g)
            except StopIteration:
                live.remove(g)


def _round_robin(gens):
    results = [None] * len(gens)
    active = list(range(len(gens)))
    while active:
        for j in list(active):
            try:
                next(gens[j])
            except StopIteration as done:
                results[j] = done.value
                active.remove(j)
    return tuple(results)


def _inproj_kernel(x_ref, nw_ref, wa_ref, wb_ref, wdt_ref,
                   qkvg_ref, zg_ref, xs_ref, bm_ref, cm_ref, qm_ref, gm_ref, dt_ref):
    tm = x_ref.shape[0]
    n_sub = tm // PROJ_SUB
    nw = nw_ref[...]

    def normed(sb):
        x = x_ref[sb * PROJ_SUB:(sb + 1) * PROJ_SUB, :]
        ms = jnp.mean(x * x, axis=-1, keepdims=True)
        return (x * lax.rsqrt(ms + EPS) * nw).astype(BF16)

    u_next = normed(0)
    for sb in range(n_sub):
        u = u_next
        if sb + 1 < n_sub:
            u_next = normed(sb + 1)
        rows = slice(sb * PROJ_SUB, (sb + 1) * PROJ_SUB)

        def scatter(ref, w_ref, c0, piece, count, first=0, post=None):
            for c in range(0, piece * count, 512):
                r = jnp.dot(u, w_ref[:, c0 + c:c0 + c + 512], preferred_element_type=F32)
                if post is not None:
                    r = post(r)
                r = r.astype(BF16)
                for j in range(512 // piece):
                    ref[first + (c + j * piece) // piece, rows, :] = r[:, j * piece:(j + 1) * piece]

        scatter(qkvg_ref, wa_ref, 0, LANES, ATT_PAIRS, post=lambda r: r * (ATT_HD ** -0.5 * LOG2E))
        scatter(qkvg_ref, wa_ref, ATT_W, LANES, 3 * ATT_PAIRS, first=ATT_PAIRS)
        scatter(zg_ref, wa_ref, 4 * ATT_W, SSD_GW, SSD_GROUPS, post=_silu)
        scatter(xs_ref, wa_ref, 4 * ATT_W + SSD_W, SSD_GW, SSD_GROUPS)
        scatter(bm_ref, wa_ref, 4 * ATT_W + 2 * SSD_W, SSD_STATE, SSD_GROUPS)
        scatter(cm_ref, wa_ref, 4 * ATT_W + 2 * SSD_W + SSD_GROUPS * SSD_STATE, SSD_STATE, SSD_GROUPS)
        scatter(qm_ref, wb_ref, 0, MEM_HD, MEM_HEADS)
        scatter(gm_ref, wb_ref, MEM_W, MEM_HD, MEM_HEADS)
        dt_ref[rows, :] = jnp.dot(u, wdt_ref[...], preferred_element_type=F32)


def _in_proj(x2d, norm_w, w_a, w_b, w_dt, tm):
    m = x2d.shape[0]
    const = lambda i: (0, 0)
    lead = lambda i: (0, i, 0)
    out_shape = (
        jax.ShapeDtypeStruct((16, m, LANES), BF16),
        jax.ShapeDtypeStruct((SSD_GROUPS, m, SSD_GW), BF16),
        jax.ShapeDtypeStruct((SSD_GROUPS, m, SSD_GW), BF16),
        jax.ShapeDtypeStruct((SSD_GROUPS, m, SSD_STATE), BF16),
        jax.ShapeDtypeStruct((SSD_GROUPS, m, SSD_STATE), BF16),
        jax.ShapeDtypeStruct((MEM_HEADS, m, MEM_HD), BF16),
        jax.ShapeDtypeStruct((MEM_HEADS, m, MEM_HD), BF16),
        jax.ShapeDtypeStruct((m, LANES), F32),
    )
    out_specs = (
        pl.BlockSpec((16, tm, LANES), lead),
        pl.BlockSpec((SSD_GROUPS, tm, SSD_GW), lead),
        pl.BlockSpec((SSD_GROUPS, tm, SSD_GW), lead),
        pl.BlockSpec((SSD_GROUPS, tm, SSD_STATE), lead),
        pl.BlockSpec((SSD_GROUPS, tm, SSD_STATE), lead),
        pl.BlockSpec((MEM_HEADS, tm, MEM_HD), lead),
        pl.BlockSpec((MEM_HEADS, tm, MEM_HD), lead),
        pl.BlockSpec((tm, LANES), lambda i: (i, 0)),
    )
    return pl.pallas_call(
        _inproj_kernel,
        grid=(m // tm,),
        in_specs=[
            pl.BlockSpec((tm, D_MODEL), lambda i: (i, 0)),
            pl.BlockSpec((1, D_MODEL), const),
            pl.BlockSpec((D_MODEL, DT_COL0), const, pipeline_mode=pl.Buffered(1)),
            pl.BlockSpec((D_MODEL, 2 * MEM_W), const, pipeline_mode=pl.Buffered(1)),
            pl.BlockSpec((D_MODEL, LANES), const, pipeline_mode=pl.Buffered(1)),
        ],
        out_specs=out_specs,
        out_shape=out_shape,
        compiler_params=pltpu.CompilerParams(
            dimension_semantics=("arbitrary",), vmem_limit_bytes=VMEM_LIMIT),
        name="in_proj",
    )(x2d, norm_w, w_a, w_b, w_dt)


def _moba_kernel(q_ref, k_ref, v_ref, g_ref, kc_ref, o_ref):
    t_len = q_ref.shape[0]
    nb = t_len // MOBA_BLOCK
    half = LANES // 2

    q = q_ref[...]
    k = k_ref[...]
    v = v_ref[...]
    kc = kc_ref[...]
    lo = lax.broadcasted_iota(jnp.int32, (t_len, LANES), 1) < half

    kmean = jnp.mean(k.astype(F32).reshape(nb, MOBA_BLOCK, LANES), axis=1)
    lo8 = lax.broadcasted_iota(jnp.int32, (nb, LANES), 1) < half
    terms = []
    for km in (jnp.where(lo8, kmean, 0.0), jnp.where(lo8, 0.0, kmean)):
        hi = km.astype(BF16).astype(F32)
        terms += [hi, km - hi]
    kstack = jnp.concatenate(terms, axis=0).astype(BF16)
    gate_t = lax.dot_general(kstack, q, _NT, preferred_element_type=F32)

    tpos = lax.broadcasted_iota(jnp.int32, (nb, t_len), 1)
    qblk = tpos // MOBA_BLOCK
    jidx = lax.broadcasted_iota(jnp.int32, (nb, t_len), 0)

    def select_bias(gate):
        cnt = jnp.zeros((nb, t_len), jnp.int32)
        for i in range(nb):
            gi = gate[i:i + 1, :]
            beats = (gi > gate) | ((gi == gate) & (i < jidx))
            cnt = cnt + jnp.where(beats & (i < qblk), 1, 0)
        sel = ((cnt < MOBA_TOPK) & (jidx < qblk)) | (jidx == qblk)
        return jnp.where(sel, 0.0, NEG)

    bias0 = select_bias(gate_t[0:nb] + gate_t[nb:2 * nb])
    bias1 = select_bias(gate_t[2 * nb:3 * nb] + gate_t[3 * nb:4 * nb])
    crow = lax.broadcasted_iota(jnp.int32, (8, t_len), 0)
    l2e = jnp.zeros((8, t_len), F32)
    for i, term in enumerate(_LOG2E_TERMS + _LOG2E_TERMS):
        l2e = jnp.where(crow == i, term, l2e)
    pad = jnp.zeros((half - nb - 8, t_len), F32)
    aug = jnp.concatenate([bias1, l2e, pad, bias0, l2e, pad], axis=0).T.astype(BF16)

    qa = (jnp.where(lo, q, aug), jnp.where(lo, aug, q))
    ka = (jnp.where(lo, k, kc), jnp.where(lo, kc, k))

    v_t = v.astype(F32).T.astype(BF16)
    ones_rows = (lax.broadcasted_iota(jnp.int32, (16, t_len), 0) == 0).astype(BF16)
    v_aug = [jnp.concatenate([v_t[h * ATT_HD:(h + 1) * ATT_HD], ones_rows], axis=0) for h in range(2)]
    key_i = lax.broadcasted_iota(jnp.int32, (MOBA_BLOCK, MOBA_BLOCK), 0)
    query_i = lax.broadcasted_iota(jnp.int32, (MOBA_BLOCK, MOBA_BLOCK), 1)
    causal_t = key_i <= query_i

    def head_task(n, h, sink):
        r0 = n * MOBA_BLOCK
        width = r0 + MOBA_BLOCK
        s = lax.dot_general(ka[h][:width], qa[h][r0:width], _NT, preferred_element_type=F32)
        yield
        s_diag = jnp.where(causal_t, s[r0:], NEG)
        s = s_diag if n == 0 else jnp.concatenate([s[:r0], s_diag], axis=0)
        m = jnp.max(s, axis=0, keepdims=True)
        yield
        p = jnp.exp2(s - m).astype(BF16)
        yield
        o_t = jnp.dot(v_aug[h][:, :width], p, preferred_element_type=F32)
        sink[h] = o_t[:ATT_HD] / o_t[ATT_HD:ATT_HD + 1]
        if h == 1:
            o = jnp.concatenate([sink[0], sink[1]], axis=0).T
            g = g_ref[r0:width, :].astype(F32)
            o_ref[r0:width, :] = (o * _silu(g)).astype(BF16)

    sinks = [dict() for _ in range(nb)]
    _pipeline([head_task(n, h, sinks[n]) for n in range(nb) for h in range(2)])


def _moba(qkvg, kc, batch, t_len):
    m = batch * t_len
    blk = lambda kind: pl.BlockSpec((None, t_len, LANES), lambda b, hp: (kind * ATT_PAIRS + hp, b, 0))
    return pl.pallas_call(
        _moba_kernel,
        grid=(batch, ATT_PAIRS),
        in_specs=[blk(0), blk(1), blk(2), blk(3),
                  pl.BlockSpec((None, t_len, LANES), lambda b, hp: (hp, 0, 0))],
        out_specs=pl.BlockSpec((None, t_len, LANES), lambda b, hp: (hp, b, 0)),
        out_shape=jax.ShapeDtypeStruct((ATT_PAIRS, m, LANES), BF16),
        compiler_params=pltpu.CompilerParams(
            dimension_semantics=("arbitrary", "arbitrary"), vmem_limit_bytes=VMEM_LIMIT),
        name="moba",
    )(qkvg, qkvg, qkvg, qkvg, kc)


def _moba_key_table(t_len):
    pos = np.arange(t_len)
    blk = pos // MOBA_BLOCK
    off = pos % MOBA_BLOCK
    slopes = np.exp2(-8.0 * np.arange(1, ATT_HEADS + 1, dtype=np.float32) / ATT_HEADS)
    onehot = (blk[:, None] == np.arange(8)[None, :]).astype(np.float32)
    half = LANES // 2

    def head_table(h):
        in_blk = (slopes[h] * off.astype(np.float32))[:, None]
        blk_start = (slopes[h] * MOBA_BLOCK * blk.astype(np.float32))[:, None]
        cols = [onehot, in_blk, in_blk, in_blk, blk_start, blk_start, blk_start,
                np.zeros((t_len, half - 14), np.float32)]
        return np.concatenate(cols, axis=1)

    tables = [np.concatenate([head_table(2 * hp + 1), head_table(2 * hp)], axis=1)
              for hp in range(ATT_PAIRS)]
    return jnp.asarray(np.stack(tables), BF16)


def _ssd_kernel(xs_ref, bm_ref, cm_ref, zg_ref, dt_ref, shift_ref, cw_ref, cb_ref,
                dtb_ref, alog_ref, dskip_ref, nw_ref, o_ref):
    t_len = xs_ref.shape[1]
    nchunks = t_len // SSD_CHUNK
    conv_width = SSD_GW + 2 * SSD_STATE
    halo_rows = 8

    dtb = dtb_ref[...]
    hrow = lax.broadcasted_iota(jnp.int32, (LANES, SSD_GW), 0)
    lcol = lax.broadcasted_iota(jnp.int32, (LANES, SSD_GW), 1)
    li = lax.broadcasted_iota(jnp.int32, (SSD_CHUNK, SSD_CHUNK), 0)
    si = lax.broadcasted_iota(jnp.int32, (SSD_CHUNK, SSD_CHUNK), 1)
    tri = li >= si
    lmat = tri.astype(BF16)
    head_of_lane = lax.broadcasted_iota(jnp.int32, (SSD_CHUNK, SSD_GW), 1) // SSD_HD

    a_es = [-jnp.exp(alog_ref[j]) * LOG2E for j in range(SSD_STREAMS)]
    expands = [(hrow == SSD_HPG * (pl.program_id(1) * SSD_STREAMS + j) + lcol // SSD_HD).astype(BF16)
               for j in range(SSD_STREAMS)]

    def stream_chunk(j, rows, h_t, halo):
        cw = cw_ref[j]
        a_e, expand = a_es[j], expands[j]

        dt = jnp.logaddexp(dt_ref[rows, :] + dtb, 0.0)
        dte = None
        for term in _split2(dt):
            part = jnp.dot(term, expand, preferred_element_type=F32)
            dte = part if dte is None else dte + part
        yield
        raw = jnp.concatenate([xs_ref[j, rows, :], bm_ref[j, rows, :], cm_ref[j, rows, :]], axis=1)
        shifted = jnp.dot(shift_ref[...], raw, preferred_element_type=F32)
        raw_f = raw.astype(F32)
        da = dte * a_e
        cs = None
        for term in _split2(da):
            part = jnp.dot(lmat, term, preferred_element_type=F32)
            cs = part if cs is None else cs + part
        zgate = zg_ref[j, rows, :].astype(F32)
        yield
        cs_t = cs.T
        cs_last = cs[SSD_CHUNK - 1:SSD_CHUNK, :]
        decays = []
        for k in range(SSD_HPG):
            colb = jnp.broadcast_to(cs[:, SSD_HD * k:SSD_HD * k + 1], (SSD_CHUNK, SSD_CHUNK))
            rowb = cs_t[SSD_HD * k:SSD_HD * k + 1, :]
            decays.append(jnp.exp2(jnp.where(tri, colb - rowb, NEG)))
        to_end = jnp.exp2(cs_last - cs)
        from_start = jnp.exp2(cs)
        yield
        hwin = jnp.concatenate([halo, jnp.zeros_like(halo)], axis=0)
        acc = cb_ref[j] + raw_f * cw[SSD_CONV - 1:SSD_CONV, :]
        edge = None
        for s in range(1, SSD_CONV):
            tap = cw[SSD_CONV - 1 - s:SSD_CONV - s, :]
            acc = acc + shifted[(s - 1) * SSD_CHUNK:s * SSD_CHUNK] * tap
            part = pltpu.roll(hwin, s, axis=0)[halo_rows:] * tap
            edge = part if edge is None else edge + part
        acc = jnp.concatenate([acc[:halo_rows] + edge, acc[halo_rows:]], axis=0)
        xbc = _silu(acc)
        xs = xbc[:, :SSD_GW]
        bm = xbc[:, SSD_GW:SSD_GW + SSD_STATE]
        cm = xbc[:, SSD_GW + SSD_STATE:]
        xdt = xs * dte
        xdt_b = xdt.astype(BF16)
        cm_b = cm.astype(BF16)
        cb = lax.dot_general(cm_b, bm.astype(BF16), _NT, preferred_element_type=F32)
        y_off = jnp.dot(cm_b, h_t.astype(BF16), preferred_element_type=F32)
        st = jnp.dot(bm.T.astype(BF16), (xdt * to_end).astype(BF16), preferred_element_type=F32)
        yield
        ws = [(cb * decays[k]).astype(BF16) for k in range(SSD_HPG)]
        xm = [jnp.where(head_of_lane == k, xdt_b, jnp.zeros_like(xdt_b)) for k in range(SSD_HPG)]
        y = jnp.dot(jnp.concatenate(ws, axis=1), jnp.concatenate(xm, axis=0),
                    preferred_element_type=F32)
        yield
        y = y + y_off * from_start + dskip_ref[j] * xs
        gated = y * zgate
        var = jnp.mean(gated * gated, axis=-1, keepdims=True)
        o_ref[j, rows, :] = (gated * lax.rsqrt(var + EPS) * nw_ref[j]).astype(BF16)
        return h_t * jnp.exp2(cs_last) + st, raw_f[SSD_CHUNK - halo_rows:, :]

    def chunk(c, carry):
        rows = pl.ds(pl.multiple_of(c * SSD_CHUNK, SSD_CHUNK), SSD_CHUNK)
        return _round_robin([stream_chunk(j, rows, *carry[j]) for j in range(SSD_STREAMS)])

    init = (jnp.zeros((SSD_STATE, SSD_GW), F32), jnp.zeros((halo_rows, conv_width), F32))
    lax.fori_loop(0, nchunks, chunk, (init,) * SSD_STREAMS)


def _ssd(xs, bm, cm, zg, dt, conv_w, conv_b, dt_bias, a_log, d_skip, norm_w, batch, t_len):
    m = batch * t_len
    g_blk = lambda w: pl.BlockSpec((SSD_STREAMS, t_len, w), lambda b, g: (g, b, 0))
    g_par = lambda r, w: pl.BlockSpec((SSD_STREAMS, r, w), lambda b, g: (g, 0, 0))
    bc_w = SSD_GROUPS * SSD_STATE

    def per_group(a):
        r = a.shape[0]
        return jnp.concatenate([a[:, :SSD_W].reshape(r, SSD_GROUPS, SSD_GW),
                                a[:, SSD_W:SSD_W + bc_w].reshape(r, SSD_GROUPS, SSD_STATE),
                                a[:, SSD_W + bc_w:].reshape(r, SSD_GROUPS, SSD_STATE)], axis=2).transpose(1, 0, 2)

    dtb = jnp.pad(dt_bias, (0, LANES - SSD_HEADS)).reshape(1, LANES)
    per_lane = lambda a: jnp.repeat(a, SSD_HD).reshape(SSD_GROUPS, 1, SSD_GW)
    conv_width = SSD_GW + 2 * SSD_STATE
    t_out = np.arange((SSD_CONV - 1) * SSD_CHUNK)[:, None]
    shift_table = jnp.asarray(t_out % SSD_CHUNK - (t_out // SSD_CHUNK + 1) == np.arange(SSD_CHUNK)[None, :], BF16)
    return pl.pallas_call(
        _ssd_kernel,
        grid=(batch, SSD_GROUPS // SSD_STREAMS),
        in_specs=[g_blk(SSD_GW), g_blk(SSD_STATE), g_blk(SSD_STATE), g_blk(SSD_GW),
                  pl.BlockSpec((t_len, LANES), lambda b, g: (b, 0)),
                  pl.BlockSpec(((SSD_CONV - 1) * SSD_CHUNK, SSD_CHUNK), lambda b, g: (0, 0)),
                  g_par(SSD_CONV, conv_width), g_par(1, conv_width),
                  pl.BlockSpec((1, LANES), lambda b, g: (0, 0)),
                  g_par(1, SSD_GW), g_par(1, SSD_GW), g_par(1, SSD_GW)],
        out_specs=g_blk(SSD_GW),
        out_shape=jax.ShapeDtypeStruct((SSD_GROUPS, m, SSD_GW), BF16),
        compiler_params=pltpu.CompilerParams(
            dimension_semantics=("arbitrary", "arbitrary"), vmem_limit_bytes=VMEM_LIMIT),
        name="ssd",
    )(xs, bm, cm, zg, dt, shift_table, per_group(conv_w), per_group(conv_b.reshape(1, SSD_XBC)), dtb,
      per_lane(a_log), per_lane(d_skip), norm_w.reshape(SSD_GROUPS, 1, SSD_GW))


def _mem_kernel(mem_ref, mnw_ref, wkv_ref, q_ref, g_ref, o_ref, *, q_chunk):
    t_len = q_ref.shape[1]
    mem = mem_ref[...]
    ms = jnp.mean(mem * mem, axis=-1, keepdims=True)
    mem_n = (mem * lax.rsqrt(ms + EPS) * mnw_ref[...]).astype(BF16)
    kv = jnp.dot(mem_n, wkv_ref[...], preferred_element_type=F32).astype(BF16)
    scale = MEM_HD ** -0.5
    for h in range(MEM_HEADS):
        k = kv[:, h * MEM_HD:(h + 1) * MEM_HD]
        v = kv[:, MEM_W + h * MEM_HD:MEM_W + (h + 1) * MEM_HD]
        for r0 in range(0, t_len, q_chunk):
            s = lax.dot_general(q_ref[h, r0:r0 + q_chunk, :], k, _NT, preferred_element_type=F32) * scale
            m = jnp.max(s, axis=-1, keepdims=True)
            p = jnp.exp(s - m)
            l = jnp.sum(p, axis=-1, keepdims=True)
            o = jnp.dot(p.astype(BF16), v, preferred_element_type=F32) / l
            g = g_ref[h, r0:r0 + q_chunk, :].astype(F32)
            o_ref[h, r0:r0 + q_chunk, :] = (o * _silu(g)).astype(BF16)


def _mem_attn(mem2d, mem_norm_w, w_kv, qm, gm, batch, t_len, mem_len):
    m = batch * t_len
    return pl.pallas_call(
        functools.partial(_mem_kernel, q_chunk=512),
        grid=(batch,),
        in_specs=[pl.BlockSpec((mem_len, D_MODEL), lambda b: (b, 0)),
                  pl.BlockSpec((1, D_MODEL), lambda b: (0, 0)),
                  pl.BlockSpec((D_MODEL, 2 * MEM_W), lambda b: (0, 0), pipeline_mode=pl.Buffered(1)),
                  pl.BlockSpec((MEM_HEADS, t_len, MEM_HD), lambda b: (0, b, 0)),
                  pl.BlockSpec((MEM_HEADS, t_len, MEM_HD), lambda b: (0, b, 0))],
        out_specs=pl.BlockSpec((MEM_HEADS, t_len, MEM_HD), lambda b: (0, b, 0)),
        out_shape=jax.ShapeDtypeStruct((MEM_HEADS, m, MEM_HD), BF16),
        compiler_params=pltpu.CompilerParams(
            dimension_semantics=("arbitrary",), vmem_limit_bytes=VMEM_LIMIT),
        name="mem_attn",
    )(mem2d, mem_norm_w, w_kv, qm, gm)


def _outproj_kernel(oatt_ref, ossd_ref, omem_ref, x_ref, w_ref, fnw_ref, o_ref):
    tm = x_ref.shape[0]
    n_sub = tm // PROJ_SUB
    fnw = fnw_ref[...]
    refs = ([(oatt_ref, i) for i in range(ATT_PAIRS)] + [(ossd_ref, i) for i in range(SSD_GROUPS)]
            + [(omem_ref, i) for i in range(MEM_HEADS)])
    h_prev = None
    for sb in range(n_sub + 1):
        if sb < n_sub:
            rows = slice(sb * PROJ_SUB, (sb + 1) * PROJ_SUB)
            mixed = jnp.concatenate([ref[i, rows, :] for ref, i in refs], axis=-1)
            h = x_ref[rows, :] + jnp.dot(mixed, w_ref[...], preferred_element_type=F32)
        if h_prev is not None:
            ms = jnp.mean(h_prev * h_prev, axis=-1, keepdims=True)
            o_ref[(sb - 1) * PROJ_SUB:sb * PROJ_SUB, :] = h_prev * lax.rsqrt(ms + EPS) * fnw
        h_prev = h


def _out_proj(oatt, ossd, omem, x2d, w_out, final_norm_w, tm):
    m = x2d.shape[0]
    lead = lambda i: (0, i, 0)
    return pl.pallas_call(
        _outproj_kernel,
        grid=(m // tm,),
        in_specs=[pl.BlockSpec((ATT_PAIRS, tm, LANES), lead),
                  pl.BlockSpec((SSD_GROUPS, tm, SSD_GW), lead),
                  pl.BlockSpec((MEM_HEADS, tm, MEM_HD), lead),
                  pl.BlockSpec((tm, D_MODEL), lambda i: (i, 0)),
                  pl.BlockSpec((MIX_W, D_MODEL), lambda i: (0, 0), pipeline_mode=pl.Buffered(1)),
                  pl.BlockSpec((1, D_MODEL), lambda i: (0, 0))],
        out_specs=pl.BlockSpec((tm, D_MODEL), lambda i: (i, 0)),
        out_shape=jax.ShapeDtypeStruct((m, D_MODEL), F32),
        compiler_params=pltpu.CompilerParams(
            dimension_semantics=("arbitrary",), vmem_limit_bytes=VMEM_LIMIT),
        name="out_proj",
    )(oatt, ossd, omem, x2d, w_out, final_norm_w)


def kernel(x, mem, norm_w, w_in, conv_w, conv_b, dt_bias, a_log, d_skip, ssd_norm_w,
           mem_norm_w, w_mem_kv, w_out, final_norm_w):
    batch, t_len, d_model = x.shape
    mem_len = mem.shape[1]
    depth = norm_w.shape[0]
    assert d_model == D_MODEL and depth == 1 and t_len % MOBA_BLOCK == 0 and t_len // MOBA_BLOCK == 8
    x2d = x.reshape(batch * t_len, D_MODEL)
    mem2d = mem.reshape(batch * mem_len, D_MODEL)
    w = w_in[0]
    w_a = w[:, :DT_COL0].astype(BF16)
    w_b = w[:, DT_COL0 + SSD_HEADS:].astype(BF16)
    w_dt = jnp.pad(w[:, DT_COL0:DT_COL0 + SSD_HEADS], ((0, 0), (0, LANES - SSD_HEADS))).astype(BF16)

    qkvg, zg, xs, bm, cm, qm, gm, dt = _in_proj(x2d, norm_w[0].reshape(1, D_MODEL), w_a, w_b, w_dt, tm=PROJ_TILE)
    o_att = _moba(qkvg, _moba_key_table(t_len), batch, t_len)
    o_ssd = _ssd(xs, bm, cm, zg, dt, conv_w[0], conv_b[0], dt_bias[0], a_log[0], d_skip[0],
                 ssd_norm_w[0], batch, t_len)
    o_mem = _mem_attn(mem2d, mem_norm_w[0].reshape(1, D_MODEL), w_mem_kv[0].astype(BF16), qm, gm,
                      batch, t_len, mem_len)
    out = _out_proj(o_att, o_ssd, o_mem, x2d, w_out[0].astype(BF16), final_norm_w.reshape(1, D_MODEL), tm=PROJ_TILE)
    return out.reshape(batch, t_len, D_MODEL)
```

```python
import functools

import jax
import jax.numpy as jnp
import numpy as np
from jax import lax
from jax.experimental import pallas as pl
from jax.experimental.pallas import tpu as pltpu

F32 = jnp.float32
BF16 = jnp.bfloat16

D_MODEL = 1024
ATT_HEADS = 8
ATT_HD = 64
ATT_W = ATT_HEADS * ATT_HD
ATT_PAIRS = ATT_HEADS // 2
MOBA_BLOCK = 256
MOBA_TOPK = 3
SSD_W = 1024
SSD_HD = 64
SSD_HEADS = 16
SSD_GROUPS = 4
SSD_HPG = SSD_HEADS // SSD_GROUPS
SSD_GW = SSD_W // SSD_GROUPS
SSD_STREAMS = 4
SSD_STATE = 128
SSD_CONV = 4
SSD_CHUNK = 256
SSD_XBC = SSD_W + 2 * SSD_GROUPS * SSD_STATE
MEM_HEADS = 4
MEM_HD = 128
MEM_W = MEM_HEADS * MEM_HD
MIX_W = ATT_W + SSD_W + MEM_W
EPS = 1e-6
NEG = -1e30
LOG2E = 1.4426950408889634
_LOG2E_TERMS = (1.4453125, -0.00262451171875, 7.063150405883789e-06)
LANES = 128
PROJ_TILE = 1024
PROJ_SUB = 256
DT_COL0 = 4 * ATT_W + SSD_W + SSD_XBC
VMEM_LIMIT = 56 * 1024 * 1024

_NT = (((1,), (1,)), ((), ()))


def _split2(a):
    hi = a.astype(BF16)
    lo = (a - hi.astype(F32)).astype(BF16)
    return hi, lo


def _silu(a):
    return a * jax.nn.sigmoid(a)


def _pipeline(gens):
    live = []
    pending = list(gens)
    while pending or live:
        if pending:
            live.append(pending.pop(0))
        for g in list(reversed(live)):
            try:
                next(g)
            except StopIteration:
                live.remove(g)


def _round_robin(gens):
    results = [None] * len(gens)
    active = list(range(len(gens)))
    while active:
        for j in list(active):
            try:
                next(gens[j])
            except StopIteration as done:
                results[j] = done.value
                active.remove(j)
    return tuple(results)


def _inproj_kernel(x_ref, nw_ref, wa_ref, wb_ref, wdt_ref,
                   qkvg_ref, zg_ref, xs_ref, bm_ref, cm_ref, qm_ref, gm_ref, dt_ref):
    tm = x_ref.shape[0]
    n_sub = tm // PROJ_SUB
    nw = nw_ref[...]

    def normed(sb):
        x = x_ref[sb * PROJ_SUB:(sb + 1) * PROJ_SUB, :]
        ms = jnp.mean(x * x, axis=-1, keepdims=True)
        return (x * lax.rsqrt(ms + EPS) * nw).astype(BF16)

    u_next = normed(0)
    for sb in range(n_sub):
        u = u_next
        if sb + 1 < n_sub:
            u_next = normed(sb + 1)
        rows = slice(sb * PROJ_SUB, (sb + 1) * PROJ_SUB)

        def scatter(ref, w_ref, c0, piece, count, first=0, post=None):
            for c in range(0, piece * count, 512):
                r = jnp.dot(u, w_ref[:, c0 + c:c0 + c + 512], preferred_element_type=F32)
                if post is not None:
                    r = post(r)
                r = r.astype(BF16)
                for j in range(512 // piece):
                    ref[first + (c + j * piece) // piece, rows, :] = r[:, j * piece:(j + 1) * piece]

        scatter(qkvg_ref, wa_ref, 0, LANES, ATT_PAIRS, post=lambda r: r * (ATT_HD ** -0.5 * LOG2E))
        scatter(qkvg_ref, wa_ref, ATT_W, LANES, 3 * ATT_PAIRS, first=ATT_PAIRS)
        scatter(zg_ref, wa_ref, 4 * ATT_W, SSD_GW, SSD_GROUPS, post=_silu)
        scatter(xs_ref, wa_ref, 4 * ATT_W + SSD_W, SSD_GW, SSD_GROUPS)
        scatter(bm_ref, wa_ref, 4 * ATT_W + 2 * SSD_W, SSD_STATE, SSD_GROUPS)
        scatter(cm_ref, wa_ref, 4 * ATT_W + 2 * SSD_W + SSD_GROUPS * SSD_STATE, SSD_STATE, SSD_GROUPS)
        scatter(qm_ref, wb_ref, 0, MEM_HD, MEM_HEADS)
        scatter(gm_ref, wb_ref, MEM_W, MEM_HD, MEM_HEADS)
        dt_ref[rows, :] = jnp.dot(u, wdt_ref[...], preferred_element_type=F32)


def _in_proj(x2d, norm_w, w_a, w_b, w_dt, tm):
    m = x2d.shape[0]
    const = lambda i: (0, 0)
    lead = lambda i: (0, i, 0)
    out_shape = (
        jax.ShapeDtypeStruct((16, m, LANES), BF16),
        jax.ShapeDtypeStruct((SSD_GROUPS, m, SSD_GW), BF16),
        jax.ShapeDtypeStruct((SSD_GROUPS, m, SSD_GW), BF16),
        jax.ShapeDtypeStruct((SSD_GROUPS, m, SSD_STATE), BF16),
        jax.ShapeDtypeStruct((SSD_GROUPS, m, SSD_STATE), BF16),
        jax.ShapeDtypeStruct((MEM_HEADS, m, MEM_HD), BF16),
        jax.ShapeDtypeStruct((MEM_HEADS, m, MEM_HD), BF16),
        jax.ShapeDtypeStruct((m, LANES), F32),
    )
    out_specs = (
        pl.BlockSpec((16, tm, LANES), lead),
        pl.BlockSpec((SSD_GROUPS, tm, SSD_GW), lead),
        pl.BlockSpec((SSD_GROUPS, tm, SSD_GW), lead),
        pl.BlockSpec((SSD_GROUPS, tm, SSD_STATE), lead),
        pl.BlockSpec((SSD_GROUPS, tm, SSD_STATE), lead),
        pl.BlockSpec((MEM_HEADS, tm, MEM_HD), lead),
        pl.BlockSpec((MEM_HEADS, tm, MEM_HD), lead),
        pl.BlockSpec((tm, LANES), lambda i: (i, 0)),
    )
    return pl.pallas_call(
        _inproj_kernel,
        grid=(m // tm,),
        in_specs=[
            pl.BlockSpec((tm, D_MODEL), lambda i: (i, 0)),
            pl.BlockSpec((1, D_MODEL), const),
            pl.BlockSpec((D_MODEL, DT_COL0), const, pipeline_mode=pl.Buffered(1)),
            pl.BlockSpec((D_MODEL, 2 * MEM_W), const, pipeline_mode=pl.Buffered(1)),
            pl.BlockSpec((D_MODEL, LANES), const, pipeline_mode=pl.Buffered(1)),
        ],
        out_specs=out_specs,
        out_shape=out_shape,
        compiler_params=pltpu.CompilerParams(
            dimension_semantics=("arbitrary",), vmem_limit_bytes=VMEM_LIMIT),
        name="in_proj",
    )(x2d, norm_w, w_a, w_b, w_dt)


def _moba_kernel(q_ref, k_ref, v_ref, g_ref, kc_ref, qc_ref, o_ref):
    t_len = q_ref.shape[0]
    nb = t_len // MOBA_BLOCK
    half = LANES // 2
    t_sel = min(MOBA_TOPK + 1, nb) * MOBA_BLOCK
    n_sel = t_len - t_sel

    k = k_ref[...]
    kc = kc_ref[...]
    lo = lax.broadcasted_iota(jnp.int32, (t_len, LANES), 1) < half
    ka = (jnp.where(lo, k, kc), jnp.where(lo, kc, k))
    lo_blk = lax.broadcasted_iota(jnp.int32, (MOBA_BLOCK, LANES), 1) < half
    key_i = lax.broadcasted_iota(jnp.int32, (MOBA_BLOCK, MOBA_BLOCK), 0)
    query_i = lax.broadcasted_iota(jnp.int32, (MOBA_BLOCK, MOBA_BLOCK), 1)
    causal_t = key_i <= query_i
    memo = {}

    def selection_lanes():
        if "aug" in memo:
            return memo["aug"]
        kmean = jnp.mean(k.astype(F32).reshape(nb, MOBA_BLOCK, LANES), axis=1)
        lo8 = lax.broadcasted_iota(jnp.int32, (nb, LANES), 1) < half
        terms = []
        for km in (jnp.where(lo8, kmean, 0.0), jnp.where(lo8, 0.0, kmean)):
            hi = km.astype(BF16).astype(F32)
            terms += [hi, km - hi]
        kstack = jnp.concatenate(terms, axis=0).astype(BF16)
        gate_t = lax.dot_general(kstack, q_ref[...], _NT, preferred_element_type=F32)[:, t_sel:]

        qblk = (lax.broadcasted_iota(jnp.int32, (nb, n_sel), 1) + t_sel) // MOBA_BLOCK
        jidx = lax.broadcasted_iota(jnp.int32, (nb, n_sel), 0)

        def select_bias(gate):
            cnt = jnp.zeros((nb, n_sel), jnp.int32)
            for i in range(nb):
                gi = gate[i:i + 1, :]
                beats = (gi > gate) | ((gi == gate) & (i < jidx))
                cnt = cnt + jnp.where(beats & (i < qblk), 1, 0)
            sel = ((cnt < MOBA_TOPK) & (jidx < qblk)) | (jidx == qblk)
            return jnp.where(sel, 0.0, NEG)

        bias0 = select_bias(gate_t[0:nb] + gate_t[nb:2 * nb])
        bias1 = select_bias(gate_t[2 * nb:3 * nb] + gate_t[3 * nb:4 * nb])
        crow = lax.broadcasted_iota(jnp.int32, (8, n_sel), 0)
        l2e = jnp.zeros((8, n_sel), F32)
        for i, term in enumerate(_LOG2E_TERMS + _LOG2E_TERMS):
            l2e = jnp.where(crow == i, term, l2e)
        pad = jnp.zeros((half - nb - 8, n_sel), F32)
        memo["aug"] = jnp.concatenate([bias1, l2e, pad, bias0, l2e, pad], axis=0).T.astype(BF16)
        return memo["aug"]

    def values_t():
        if "v" not in memo:
            v_t = v_ref[...].astype(F32).T.astype(BF16)
            ones_rows = (lax.broadcasted_iota(jnp.int32, (16, t_len), 0) == 0).astype(BF16)
            memo["v"] = [jnp.concatenate([v_t[h * ATT_HD:(h + 1) * ATT_HD], ones_rows], axis=0) for h in range(2)]
        return memo["v"]

    def head_task(n, h, sink):
        r0 = n * MOBA_BLOCK
        width = r0 + MOBA_BLOCK
        aug = qc_ref[...] if r0 < t_sel else selection_lanes()[r0 - t_sel:width - t_sel]
        q_blk = q_ref[r0:width, :]
        qa = jnp.where(lo_blk, q_blk, aug) if h == 0 else jnp.where(lo_blk, aug, q_blk)
        s = lax.dot_general(ka[h][:width], qa, _NT, preferred_element_type=F32)
        yield
        s_diag = jnp.where(causal_t, s[r0:], NEG)
        s = s_diag if n == 0 else jnp.concatenate([s[:r0], s_diag], axis=0)
        m = jnp.max(s, axis=0, keepdims=True)
        yield
        p = jnp.exp2(s - m).astype(BF16)
        yield
        o_t = jnp.dot(values_t()[h][:, :width], p, preferred_element_type=F32)
        sink[h] = o_t[:ATT_HD] / o_t[ATT_HD:ATT_HD + 1]
        if h == 1:
            o = jnp.concatenate([sink[0], sink[1]], axis=0).T
            g = g_ref[r0:width, :].astype(F32)
            o_ref[r0:width, :] = (o * _silu(g)).astype(BF16)

    sinks = [dict() for _ in range(nb)]
    _pipeline([head_task(n, h, sinks[n]) for n in range(nb) for h in range(2)])


def _moba(qkvg, kc, batch, t_len):
    m = batch * t_len
    qc = np.zeros((MOBA_BLOCK, LANES), np.float32)
    for i, term in enumerate(_LOG2E_TERMS + _LOG2E_TERMS):
        qc[:, 8 + i] = term
        qc[:, LANES // 2 + 8 + i] = term
    qc = jnp.asarray(qc, BF16)
    blk = lambda kind: pl.BlockSpec((None, t_len, LANES), lambda b, hp: (kind * ATT_PAIRS + hp, b, 0))
    return pl.pallas_call(
        _moba_kernel,
        grid=(batch, ATT_PAIRS),
        in_specs=[blk(0), blk(1), blk(2), blk(3),
                  pl.BlockSpec((None, t_len, LANES), lambda b, hp: (hp, 0, 0)),
                  pl.BlockSpec((MOBA_BLOCK, LANES), lambda b, hp: (0, 0))],
        out_specs=pl.BlockSpec((None, t_len, LANES), lambda b, hp: (hp, b, 0)),
        out_shape=jax.ShapeDtypeStruct((ATT_PAIRS, m, LANES), BF16),
        compiler_params=pltpu.CompilerParams(
            dimension_semantics=("arbitrary", "arbitrary"), vmem_limit_bytes=VMEM_LIMIT),
        name="moba",
    )(qkvg, qkvg, qkvg, qkvg, kc, qc)


def _moba_key_table(t_len):
    pos = np.arange(t_len)
    blk = pos // MOBA_BLOCK
    off = pos % MOBA_BLOCK
    slopes = np.exp2(-8.0 * np.arange(1, ATT_HEADS + 1, dtype=np.float32) / ATT_HEADS)
    onehot = (blk[:, None] == np.arange(8)[None, :]).astype(np.float32)
    half = LANES // 2

    def head_table(h):
        in_blk = (slopes[h] * off.astype(np.float32))[:, None]
        blk_start = (slopes[h] * MOBA_BLOCK * blk.astype(np.float32))[:, None]
        cols = [onehot, in_blk, in_blk, in_blk, blk_start, blk_start, blk_start,
                np.zeros((t_len, half - 14), np.float32)]
        return np.concatenate(cols, axis=1)

    tables = [np.concatenate([head_table(2 * hp + 1), head_table(2 * hp)], axis=1)
              for hp in range(ATT_PAIRS)]
    return jnp.asarray(np.stack(tables), BF16)


def _ssd_kernel(xs_ref, bm_ref, cm_ref, zg_ref, dt_ref, shift_ref, cw_ref, cb_ref,
                dtb_ref, alog_ref, dskip_ref, nw_ref, o_ref):
    t_len = xs_ref.shape[1]
    nchunks = t_len // SSD_CHUNK
    conv_width = SSD_GW + 2 * SSD_STATE
    halo_rows = 8

    dtb = dtb_ref[...]
    hrow = lax.broadcasted_iota(jnp.int32, (LANES, SSD_GW), 0)
    lcol = lax.broadcasted_iota(jnp.int32, (LANES, SSD_GW), 1)
    li = lax.broadcasted_iota(jnp.int32, (SSD_CHUNK, SSD_CHUNK), 0)
    si = lax.broadcasted_iota(jnp.int32, (SSD_CHUNK, SSD_CHUNK), 1)
    tri = li >= si
    lmat = tri.astype(BF16)
    head_of_lane = lax.broadcasted_iota(jnp.int32, (SSD_CHUNK, SSD_GW), 1) // SSD_HD

    a_es = [-jnp.exp(alog_ref[j]) * LOG2E for j in range(SSD_STREAMS)]
    expands = [(hrow == SSD_HPG * (pl.program_id(1) * SSD_STREAMS + j) + lcol // SSD_HD).astype(BF16)
               for j in range(SSD_STREAMS)]

    def stream_chunk(j, rows, h_t, halo):
        cw = cw_ref[j]
        a_e, expand = a_es[j], expands[j]

        dt = jnp.logaddexp(dt_ref[rows, :] + dtb, 0.0)
        dte = None
        for term in _split2(dt):
            part = jnp.dot(term, expand, preferred_element_type=F32)
            dte = part if dte is None else dte + part
        yield
        raw = jnp.concatenate([xs_ref[j, rows, :], bm_ref[j, rows, :], cm_ref[j, rows, :]], axis=1)
        shifted = jnp.dot(shift_ref[...], raw, preferred_element_type=F32)
        raw_f = raw.astype(F32)
        da = dte * a_e
        cs = None
        for term in _split2(da):
            part = jnp.dot(lmat, term, preferred_element_type=F32)
            cs = part if cs is None else cs + part
        zgate = zg_ref[j, rows, :].astype(F32)
        yield
        cs_t = cs.T
        cs_last = cs[SSD_CHUNK - 1:SSD_CHUNK, :]
        decays = []
        for k in range(SSD_HPG):
            colb = jnp.broadcast_to(cs[:, SSD_HD * k:SSD_HD * k + 1], (SSD_CHUNK, SSD_CHUNK))
            rowb = cs_t[SSD_HD * k:SSD_HD * k + 1, :]
            decays.append(jnp.exp2(jnp.where(tri, colb - rowb, NEG)))
        to_end = jnp.exp2(cs_last - cs)
        from_start = jnp.exp2(cs)
        yield
        hwin = jnp.concatenate([halo, jnp.zeros_like(halo)], axis=0)
        acc = cb_ref[j] + raw_f * cw[SSD_CONV - 1:SSD_CONV, :]
        edge = None
        for s in range(1, SSD_CONV):
            tap = cw[SSD_CONV - 1 - s:SSD_CONV - s, :]
            acc = acc + shifted[(s - 1) * SSD_CHUNK:s * SSD_CHUNK] * tap
            part = pltpu.roll(hwin, s, axis=0)[halo_rows:] * tap
            edge = part if edge is None else edge + part
        acc = jnp.concatenate([acc[:halo_rows] + edge, acc[halo_rows:]], axis=0)
        xbc = _silu(acc)
        xs = xbc[:, :SSD_GW]
        bm = xbc[:, SSD_GW:SSD_GW + SSD_STATE]
        cm = xbc[:, SSD_GW + SSD_STATE:]
        xdt = xs * dte
        xdt_b = xdt.astype(BF16)
        cm_b = cm.astype(BF16)
        cb = lax.dot_general(cm_b, bm.astype(BF16), _NT, preferred_element_type=F32)
        y_off = jnp.dot(cm_b, h_t.astype(BF16), preferred_element_type=F32)
        st = jnp.dot(bm.T.astype(BF16), (xdt * to_end).astype(BF16), preferred_element_type=F32)
        yield
        ws = [(cb * decays[k]).astype(BF16) for k in range(SSD_HPG)]
        xm = [jnp.where(head_of_lane == k, xdt_b, jnp.zeros_like(xdt_b)) for k in range(SSD_HPG)]
        y = jnp.dot(jnp.concatenate(ws, axis=1), jnp.concatenate(xm, axis=0),
                    preferred_element_type=F32)
        yield
        y = y + y_off * from_start + dskip_ref[j] * xs
        gated = y * zgate
        var = jnp.mean(gated * gated, axis=-1, keepdims=True)
        o_ref[j, rows, :] = (gated * lax.rsqrt(var + EPS) * nw_ref[j]).astype(BF16)
        return h_t * jnp.exp2(cs_last) + st, raw_f[SSD_CHUNK - halo_rows:, :]

    def chunk(c, carry):
        rows = pl.ds(pl.multiple_of(c * SSD_CHUNK, SSD_CHUNK), SSD_CHUNK)
        return _round_robin([stream_chunk(j, rows, *carry[j]) for j in range(SSD_STREAMS)])

    init = (jnp.zeros((SSD_STATE, SSD_GW), F32), jnp.zeros((halo_rows, conv_width), F32))
    lax.fori_loop(0, nchunks, chunk, (init,) * SSD_STREAMS)


def _ssd(xs, bm, cm, zg, dt, conv_w, conv_b, dt_bias, a_log, d_skip, norm_w, batch, t_len):
    m = batch * t_len
    g_blk = lambda w: pl.BlockSpec((SSD_STREAMS, t_len, w), lambda b, g: (g, b, 0))
    g_par = lambda r, w: pl.BlockSpec((SSD_STREAMS, r, w), lambda b, g: (g, 0, 0))
    bc_w = SSD_GROUPS * SSD_STATE

    def per_group(a):
        r = a.shape[0]
        return jnp.concatenate([a[:, :SSD_W].reshape(r, SSD_GROUPS, SSD_GW),
                                a[:, SSD_W:SSD_W + bc_w].reshape(r, SSD_GROUPS, SSD_STATE),
                                a[:, SSD_W + bc_w:].reshape(r, SSD_GROUPS, SSD_STATE)], axis=2).transpose(1, 0, 2)

    dtb = jnp.pad(dt_bias, (0, LANES - SSD_HEADS)).reshape(1, LANES)
    per_lane = lambda a: jnp.repeat(a, SSD_HD).reshape(SSD_GROUPS, 1, SSD_GW)
    conv_width = SSD_GW + 2 * SSD_STATE
    t_out = np.arange((SSD_CONV - 1) * SSD_CHUNK)[:, None]
    shift_table = jnp.asarray(t_out % SSD_CHUNK - (t_out // SSD_CHUNK + 1) == np.arange(SSD_CHUNK)[None, :], BF16)
    return pl.pallas_call(
        _ssd_kernel,
        grid=(batch, SSD_GROUPS // SSD_STREAMS),
        in_specs=[g_blk(SSD_GW), g_blk(SSD_STATE), g_blk(SSD_STATE), g_blk(SSD_GW),
                  pl.BlockSpec((t_len, LANES), lambda b, g: (b, 0)),
                  pl.BlockSpec(((SSD_CONV - 1) * SSD_CHUNK, SSD_CHUNK), lambda b, g: (0, 0)),
                  g_par(SSD_CONV, conv_width), g_par(1, conv_width),
                  pl.BlockSpec((1, LANES), lambda b, g: (0, 0)),
                  g_par(1, SSD_GW), g_par(1, SSD_GW), g_par(1, SSD_GW)],
        out_specs=g_blk(SSD_GW),
        out_shape=jax.ShapeDtypeStruct((SSD_GROUPS, m, SSD_GW), BF16),
        compiler_params=pltpu.CompilerParams(
            dimension_semantics=("arbitrary", "arbitrary"), vmem_limit_bytes=VMEM_LIMIT),
        name="ssd",
    )(xs, bm, cm, zg, dt, shift_table, per_group(conv_w), per_group(conv_b.reshape(1, SSD_XBC)), dtb,
      per_lane(a_log), per_lane(d_skip), norm_w.reshape(SSD_GROUPS, 1, SSD_GW))


def _mem_kernel(mem_ref, mnw_ref, wkv_ref, q_ref, g_ref, o_ref, *, q_chunk):
    t_len = q_ref.shape[1]
    mem = mem_ref[...]
    ms = jnp.mean(mem * mem, axis=-1, keepdims=True)
    mem_n = (mem * lax.rsqrt(ms + EPS) * mnw_ref[...]).astype(BF16)
    kv = jnp.dot(mem_n, wkv_ref[...], preferred_element_type=F32)
    k_all = (kv[:, :MEM_W] * (MEM_HD ** -0.5 * LOG2E)).astype(BF16)
    v_all = kv[:, MEM_W:].astype(BF16)
    for h in range(MEM_HEADS):
        k = k_all[:, h * MEM_HD:(h + 1) * MEM_HD]
        v = v_all[:, h * MEM_HD:(h + 1) * MEM_HD]
        for r0 in range(0, t_len, q_chunk):
            s = lax.dot_general(q_ref[h, r0:r0 + q_chunk, :], k, _NT, preferred_element_type=F32)
            m = jnp.max(s, axis=-1, keepdims=True)
            p = jnp.exp2(s - m)
            l = jnp.sum(p, axis=-1, keepdims=True)
            o = jnp.dot(p.astype(BF16), v, preferred_element_type=F32) / l
            g = g_ref[h, r0:r0 + q_chunk, :].astype(F32)
            o_ref[h, r0:r0 + q_chunk, :] = (o * _silu(g)).astype(BF16)


def _mem_attn(mem2d, mem_norm_w, w_kv, qm, gm, batch, t_len, mem_len):
    m = batch * t_len
    return pl.pallas_call(
        functools.partial(_mem_kernel, q_chunk=512),
        grid=(batch,),
        in_specs=[pl.BlockSpec((mem_len, D_MODEL), lambda b: (b, 0)),
                  pl.BlockSpec((1, D_MODEL), lambda b: (0, 0)),
                  pl.BlockSpec((D_MODEL, 2 * MEM_W), lambda b: (0, 0), pipeline_mode=pl.Buffered(1)),
                  pl.BlockSpec((MEM_HEADS, t_len, MEM_HD), lambda b: (0, b, 0)),
                  pl.BlockSpec((MEM_HEADS, t_len, MEM_HD), lambda b: (0, b, 0))],
        out_specs=pl.BlockSpec((MEM_HEADS, t_len, MEM_HD), lambda b: (0, b, 0)),
        out_shape=jax.ShapeDtypeStruct((MEM_HEADS, m, MEM_HD), BF16),
        compiler_params=pltpu.CompilerParams(
            dimension_semantics=("arbitrary",), vmem_limit_bytes=VMEM_LIMIT),
        name="mem_attn",
    )(mem2d, mem_norm_w, w_kv, qm, gm)


def _outproj_kernel(oatt_ref, ossd_ref, omem_ref, x_ref, w_ref, fnw_ref, o_ref):
    tm = x_ref.shape[0]
    n_sub = tm // PROJ_SUB
    fnw = fnw_ref[...]
    refs = ([(oatt_ref, i) for i in range(ATT_PAIRS)] + [(ossd_ref, i) for i in range(SSD_GROUPS)]
            + [(omem_ref, i) for i in range(MEM_HEADS)])
    h_prev = None
    for sb in range(n_sub + 1):
        if sb < n_sub:
            rows = slice(sb * PROJ_SUB, (sb + 1) * PROJ_SUB)
            mixed = jnp.concatenate([ref[i, rows, :] for ref, i in refs], axis=-1)
            h = x_ref[rows, :] + jnp.dot(mixed, w_ref[...], preferred_element_type=F32)
        if h_prev is not None:
            ms = jnp.mean(h_prev * h_prev, axis=-1, keepdims=True)
            o_ref[(sb - 1) * PROJ_SUB:sb * PROJ_SUB, :] = h_prev * lax.rsqrt(ms + EPS) * fnw
        h_prev = h


def _out_proj(oatt, ossd, omem, x2d, w_out, final_norm_w, tm):
    m = x2d.shape[0]
    lead = lambda i: (0, i, 0)
    return pl.pallas_call(
        _outproj_kernel,
        grid=(m // tm,),
        in_specs=[pl.BlockSpec((ATT_PAIRS, tm, LANES), lead),
                  pl.BlockSpec((SSD_GROUPS, tm, SSD_GW), lead),
                  pl.BlockSpec((MEM_HEADS, tm, MEM_HD), lead),
                  pl.BlockSpec((tm, D_MODEL), lambda i: (i, 0)),
                  pl.BlockSpec((MIX_W, D_MODEL), lambda i: (0, 0), pipeline_mode=pl.Buffered(1)),
                  pl.BlockSpec((1, D_MODEL), lambda i: (0, 0))],
        out_specs=pl.BlockSpec((tm, D_MODEL), lambda i: (i, 0)),
        out_shape=jax.ShapeDtypeStruct((m, D_MODEL), F32),
        compiler_params=pltpu.CompilerParams(
            dimension_semantics=("arbitrary",), vmem_limit_bytes=VMEM_LIMIT),
        name="out_proj",
    )(oatt, ossd, omem, x2d, w_out, final_norm_w)


def kernel(x, mem, norm_w, w_in, conv_w, conv_b, dt_bias, a_log, d_skip, ssd_norm_w,
           mem_norm_w, w_mem_kv, w_out, final_norm_w):
    batch, t_len, d_model = x.shape
    mem_len = mem.shape[1]
    depth = norm_w.shape[0]
    assert d_model == D_MODEL and depth == 1 and t_len % MOBA_BLOCK == 0 and t_len // MOBA_BLOCK == 8
    x2d = x.reshape(batch * t_len, D_MODEL)
    mem2d = mem.reshape(batch * mem_len, D_MODEL)
    w = w_in[0]
    w_bf = w.astype(BF16)
    w_b = w_bf[:, DT_COL0 + SSD_HEADS:]
    w_dt = jnp.pad(w_bf[:, DT_COL0:DT_COL0 + SSD_HEADS], ((0, 0), (0, LANES - SSD_HEADS)))

    qkvg, zg, xs, bm, cm, qm, gm, dt = _in_proj(x2d, norm_w[0].reshape(1, D_MODEL), w_bf, w_b, w_dt, tm=PROJ_TILE)
    o_att = _moba(qkvg, _moba_key_table(t_len), batch, t_len)
    o_ssd = _ssd(xs, bm, cm, zg, dt, conv_w[0], conv_b[0], dt_bias[0], a_log[0], d_skip[0],
                 ssd_norm_w[0], batch, t_len)
    o_mem = _mem_attn(mem2d, mem_norm_w[0].reshape(1, D_MODEL), w_mem_kv[0].astype(BF16), qm, gm,
                      batch, t_len, mem_len)
    out = _out_proj(o_att, o_ssd, o_mem, x2d, w_out[0].astype(BF16), final_norm_w.reshape(1, D_MODEL), tm=PROJ_TILE)
    return out.reshape(batch, t_len, D_MODEL)
```

```python
import functools

import jax
import jax.numpy as jnp
import numpy as np
from jax import lax
from jax.experimental import pallas as pl
from jax.experimental.pallas import tpu as pltpu

F32 = jnp.float32
BF16 = jnp.bfloat16

D_MODEL = 1024
ATT_HEADS = 8
ATT_HD = 64
ATT_W = ATT_HEADS * ATT_HD
ATT_PAIRS = ATT_HEADS // 2
MOBA_BLOCK = 256
MOBA_TOPK = 3
MOBA_PAIRS_PER_STEP = 2
SSD_W = 1024
SSD_HD = 64
SSD_HEADS = 16
SSD_GROUPS = 4
SSD_HPG = SSD_HEADS // SSD_GROUPS
SSD_GW = SSD_W // SSD_GROUPS
SSD_STREAMS = 4
SSD_STATE = 128
SSD_CONV = 4
SSD_CHUNK = 256
SSD_XBC = SSD_W + 2 * SSD_GROUPS * SSD_STATE
MEM_HEADS = 4
MEM_HD = 128
MEM_W = MEM_HEADS * MEM_HD
MIX_W = ATT_W + SSD_W + MEM_W
EPS = 1e-6
NEG = -1e30
LOG2E = 1.4426950408889634
_LOG2E_TERMS = (1.4453125, -0.00262451171875, 7.063150405883789e-06)
LANES = 128
PROJ_TILE = 1024
PROJ_SUB = 256
DT_COL0 = 4 * ATT_W + SSD_W + SSD_XBC
VMEM_LIMIT = 56 * 1024 * 1024

_NT = (((1,), (1,)), ((), ()))


def _split2(a):
    hi = a.astype(BF16)
    lo = (a - hi.astype(F32)).astype(BF16)
    return hi, lo


def _silu(a):
    return a * jax.nn.sigmoid(a)


def _pipeline(gens):
    live = []
    pending = list(gens)
    while pending or live:
        if pending:
            live.append(pending.pop(0))
        for g in list(reversed(live)):
            try:
                next(g)
            except StopIteration:
                live.remove(g)


def _round_robin(gens):
    results = [None] * len(gens)
    active = list(range(len(gens)))
    while active:
        for j in list(active):
            try:
                next(gens[j])
            except StopIteration as done:
                results[j] = done.value
                active.remove(j)
    return tuple(results)


def _inproj_kernel(x_ref, nw_ref, wa_ref, wb_ref, wdt_ref,
                   qkvg_ref, zg_ref, xs_ref, bm_ref, cm_ref, qm_ref, gm_ref, dt_ref):
    tm = x_ref.shape[0]
    n_sub = tm // PROJ_SUB
    nw = nw_ref[...]

    def normed(sb):
        x = x_ref[sb * PROJ_SUB:(sb + 1) * PROJ_SUB, :]
        ms = jnp.mean(x * x, axis=-1, keepdims=True)
        return (x * lax.rsqrt(ms + EPS) * nw).astype(BF16)

    u_next = normed(0)
    for sb in range(n_sub):
        u = u_next
        if sb + 1 < n_sub:
            u_next = normed(sb + 1)
        rows = slice(sb * PROJ_SUB, (sb + 1) * PROJ_SUB)

        def scatter(ref, w_ref, c0, piece, count, first=0, post=None):
            for c in range(0, piece * count, 512):
                r = jnp.dot(u, w_ref[:, c0 + c:c0 + c + 512], preferred_element_type=F32)
                if post is not None:
                    r = post(r)
                r = r.astype(BF16)
                for j in range(512 // piece):
                    ref[first + (c + j * piece) // piece, rows, :] = r[:, j * piece:(j + 1) * piece]

        scatter(qkvg_ref, wa_ref, 0, LANES, ATT_PAIRS, post=lambda r: r * (ATT_HD ** -0.5 * LOG2E))
        scatter(qkvg_ref, wa_ref, ATT_W, LANES, 3 * ATT_PAIRS, first=ATT_PAIRS)
        scatter(zg_ref, wa_ref, 4 * ATT_W, SSD_GW, SSD_GROUPS, post=_silu)
        scatter(xs_ref, wa_ref, 4 * ATT_W + SSD_W, SSD_GW, SSD_GROUPS)
        scatter(bm_ref, wa_ref, 4 * ATT_W + 2 * SSD_W, SSD_STATE, SSD_GROUPS)
        scatter(cm_ref, wa_ref, 4 * ATT_W + 2 * SSD_W + SSD_GROUPS * SSD_STATE, SSD_STATE, SSD_GROUPS)
        scatter(qm_ref, wb_ref, 0, MEM_HD, MEM_HEADS)
        scatter(gm_ref, wb_ref, MEM_W, MEM_HD, MEM_HEADS)
        dt_ref[rows, :] = jnp.dot(u, wdt_ref[...], preferred_element_type=F32)


def _in_proj(x2d, norm_w, w_a, w_b, w_dt, tm):
    m = x2d.shape[0]
    const = lambda i: (0, 0)
    lead = lambda i: (0, i, 0)
    out_shape = (
        jax.ShapeDtypeStruct((16, m, LANES), BF16),
        jax.ShapeDtypeStruct((SSD_GROUPS, m, SSD_GW), BF16),
        jax.ShapeDtypeStruct((SSD_GROUPS, m, SSD_GW), BF16),
        jax.ShapeDtypeStruct((SSD_GROUPS, m, SSD_STATE), BF16),
        jax.ShapeDtypeStruct((SSD_GROUPS, m, SSD_STATE), BF16),
        jax.ShapeDtypeStruct((MEM_HEADS, m, MEM_HD), BF16),
        jax.ShapeDtypeStruct((MEM_HEADS, m, MEM_HD), BF16),
        jax.ShapeDtypeStruct((m, LANES), F32),
    )
    out_specs = (
        pl.BlockSpec((16, tm, LANES), lead),
        pl.BlockSpec((SSD_GROUPS, tm, SSD_GW), lead),
        pl.BlockSpec((SSD_GROUPS, tm, SSD_GW), lead),
        pl.BlockSpec((SSD_GROUPS, tm, SSD_STATE), lead),
        pl.BlockSpec((SSD_GROUPS, tm, SSD_STATE), lead),
        pl.BlockSpec((MEM_HEADS, tm, MEM_HD), lead),
        pl.BlockSpec((MEM_HEADS, tm, MEM_HD), lead),
        pl.BlockSpec((tm, LANES), lambda i: (i, 0)),
    )
    return pl.pallas_call(
        _inproj_kernel,
        grid=(m // tm,),
        in_specs=[
            pl.BlockSpec((tm, D_MODEL), lambda i: (i, 0)),
            pl.BlockSpec((1, D_MODEL), const),
            pl.BlockSpec((D_MODEL, DT_COL0), const, pipeline_mode=pl.Buffered(1)),
            pl.BlockSpec((D_MODEL, 2 * MEM_W), const, pipeline_mode=pl.Buffered(1)),
            pl.BlockSpec((D_MODEL, LANES), const, pipeline_mode=pl.Buffered(1)),
        ],
        out_specs=out_specs,
        out_shape=out_shape,
        compiler_params=pltpu.CompilerParams(
            dimension_semantics=("arbitrary",), vmem_limit_bytes=VMEM_LIMIT),
        name="in_proj",
    )(x2d, norm_w, w_a, w_b, w_dt)


def _moba_pair_tasks(q_ref, k_ref, v_ref, g_ref, kc_ref, qc_ref, o_ref):
    t_len = q_ref.shape[0]
    nb = t_len // MOBA_BLOCK
    half = LANES // 2
    t_sel = min(MOBA_TOPK + 1, nb) * MOBA_BLOCK
    n_sel = t_len - t_sel

    k = k_ref[...]
    kc = kc_ref[...]
    lo = lax.broadcasted_iota(jnp.int32, (t_len, LANES), 1) < half
    ka = (jnp.where(lo, k, kc), jnp.where(lo, kc, k))
    lo_blk = lax.broadcasted_iota(jnp.int32, (MOBA_BLOCK, LANES), 1) < half
    key_i = lax.broadcasted_iota(jnp.int32, (MOBA_BLOCK, MOBA_BLOCK), 0)
    query_i = lax.broadcasted_iota(jnp.int32, (MOBA_BLOCK, MOBA_BLOCK), 1)
    causal_t = key_i <= query_i
    memo = {}

    def selection_lanes():
        if "aug" in memo:
            return memo["aug"]
        kmean = jnp.mean(k.astype(F32).reshape(nb, MOBA_BLOCK, LANES), axis=1)
        lo8 = lax.broadcasted_iota(jnp.int32, (nb, LANES), 1) < half
        terms = []
        for km in (jnp.where(lo8, kmean, 0.0), jnp.where(lo8, 0.0, kmean)):
            hi = km.astype(BF16).astype(F32)
            terms += [hi, km - hi]
        kstack = jnp.concatenate(terms, axis=0).astype(BF16)
        gate_t = lax.dot_general(kstack, q_ref[...], _NT, preferred_element_type=F32)[:, t_sel:]

        qblk = (lax.broadcasted_iota(jnp.int32, (nb, n_sel), 1) + t_sel) // MOBA_BLOCK
        jidx = lax.broadcasted_iota(jnp.int32, (nb, n_sel), 0)

        def select_bias(gate):
            cnt = jnp.zeros((nb, n_sel), jnp.int32)
            for i in range(nb):
                gi = gate[i:i + 1, :]
                beats = (gi > gate) | ((gi == gate) & (i < jidx))
                cnt = cnt + jnp.where(beats & (i < qblk), 1, 0)
            sel = ((cnt < MOBA_TOPK) & (jidx < qblk)) | (jidx == qblk)
            return jnp.where(sel, 0.0, NEG)

        bias0 = select_bias(gate_t[0:nb] + gate_t[nb:2 * nb])
        bias1 = select_bias(gate_t[2 * nb:3 * nb] + gate_t[3 * nb:4 * nb])
        crow = lax.broadcasted_iota(jnp.int32, (8, n_sel), 0)
        l2e = jnp.zeros((8, n_sel), F32)
        for i, term in enumerate(_LOG2E_TERMS + _LOG2E_TERMS):
            l2e = jnp.where(crow == i, term, l2e)
        pad = jnp.zeros((half - nb - 8, n_sel), F32)
        memo["aug"] = jnp.concatenate([bias1, l2e, pad, bias0, l2e, pad], axis=0).T.astype(BF16)
        return memo["aug"]

    def values_t():
        if "v" not in memo:
            v_t = v_ref[...].astype(F32).T.astype(BF16)
            ones_rows = (lax.broadcasted_iota(jnp.int32, (16, t_len), 0) == 0).astype(BF16)
            memo["v"] = [jnp.concatenate([v_t[h * ATT_HD:(h + 1) * ATT_HD], ones_rows], axis=0) for h in range(2)]
        return memo["v"]

    def head_task(n, h, sink):
        r0 = n * MOBA_BLOCK
        width = r0 + MOBA_BLOCK
        aug = qc_ref[...] if r0 < t_sel else selection_lanes()[r0 - t_sel:width - t_sel]
        q_blk = q_ref[r0:width, :]
        qa = jnp.where(lo_blk, q_blk, aug) if h == 0 else jnp.where(lo_blk, aug, q_blk)
        s = lax.dot_general(ka[h][:width], qa, _NT, preferred_element_type=F32)
        yield
        s_diag = jnp.where(causal_t, s[r0:], NEG)
        s = s_diag if n == 0 else jnp.concatenate([s[:r0], s_diag], axis=0)
        m = jnp.max(s, axis=0, keepdims=True)
        yield
        p = jnp.exp2(s - m).astype(BF16)
        yield
        o_t = jnp.dot(values_t()[h][:, :width], p, preferred_element_type=F32)
        sink[h] = o_t[:ATT_HD] / o_t[ATT_HD:ATT_HD + 1]
        if h == 1:
            o = jnp.concatenate([sink[0], sink[1]], axis=0).T
            g = g_ref[r0:width, :].astype(F32)
            o_ref[r0:width, :] = (o * _silu(g)).astype(BF16)

    sinks = [dict() for _ in range(nb)]
    return [head_task(n, h, sinks[n]) for n in range(nb) for h in range(2)]


def _moba_kernel(q_ref, k_ref, v_ref, g_ref, kc_ref, qc_ref, o_ref):
    tasks = []
    for i in range(MOBA_PAIRS_PER_STEP):
        tasks += _moba_pair_tasks(q_ref.at[i], k_ref.at[i], v_ref.at[i], g_ref.at[i], kc_ref.at[i], qc_ref, o_ref.at[i])
    _pipeline(tasks)


def _moba(qkvg, kc, batch, t_len):
    m = batch * t_len
    qc = np.zeros((MOBA_BLOCK, LANES), np.float32)
    for i, term in enumerate(_LOG2E_TERMS + _LOG2E_TERMS):
        qc[:, 8 + i] = term
        qc[:, LANES // 2 + 8 + i] = term
    qc = jnp.asarray(qc, BF16)
    per_step = MOBA_PAIRS_PER_STEP
    steps = ATT_PAIRS // per_step
    blk = lambda kind: pl.BlockSpec((per_step, t_len, LANES), lambda b, hp: (kind * steps + hp, b, 0))
    return pl.pallas_call(
        _moba_kernel,
        grid=(batch, steps),
        in_specs=[blk(0), blk(1), blk(2), blk(3),
                  pl.BlockSpec((per_step, t_len, LANES), lambda b, hp: (hp, 0, 0)),
                  pl.BlockSpec((MOBA_BLOCK, LANES), lambda b, hp: (0, 0))],
        out_specs=pl.BlockSpec((per_step, t_len, LANES), lambda b, hp: (hp, b, 0)),
        out_shape=jax.ShapeDtypeStruct((ATT_PAIRS, m, LANES), BF16),
        compiler_params=pltpu.CompilerParams(
            dimension_semantics=("arbitrary", "arbitrary"), vmem_limit_bytes=VMEM_LIMIT),
        name="moba",
    )(qkvg, qkvg, qkvg, qkvg, kc, qc)


def _moba_key_table(t_len):
    pos = np.arange(t_len)
    blk = pos // MOBA_BLOCK
    off = pos % MOBA_BLOCK
    slopes = np.exp2(-8.0 * np.arange(1, ATT_HEADS + 1, dtype=np.float32) / ATT_HEADS)
    onehot = (blk[:, None] == np.arange(8)[None, :]).astype(np.float32)
    half = LANES // 2

    def head_table(h):
        in_blk = (slopes[h] * off.astype(np.float32))[:, None]
        blk_start = (slopes[h] * MOBA_BLOCK * blk.astype(np.float32))[:, None]
        cols = [onehot, in_blk, in_blk, in_blk, blk_start, blk_start, blk_start,
                np.zeros((t_len, half - 14), np.float32)]
        return np.concatenate(cols, axis=1)

    tables = [np.concatenate([head_table(2 * hp + 1), head_table(2 * hp)], axis=1)
              for hp in range(ATT_PAIRS)]
    return jnp.asarray(np.stack(tables), BF16)


def _ssd_kernel(xs_ref, bm_ref, cm_ref, zg_ref, dt_ref, shift_ref, cw_ref, cb_ref,
                dtb_ref, alog_ref, dskip_ref, nw_ref, o_ref):
    t_len = xs_ref.shape[1]
    nchunks = t_len // SSD_CHUNK
    conv_width = SSD_GW + 2 * SSD_STATE
    halo_rows = 8

    dtb = dtb_ref[...]
    hrow = lax.broadcasted_iota(jnp.int32, (LANES, SSD_GW), 0)
    lcol = lax.broadcasted_iota(jnp.int32, (LANES, SSD_GW), 1)
    li = lax.broadcasted_iota(jnp.int32, (SSD_CHUNK, SSD_CHUNK), 0)
    si = lax.broadcasted_iota(jnp.int32, (SSD_CHUNK, SSD_CHUNK), 1)
    tri = li >= si
    lmat = tri.astype(BF16)
    head_of_lane = lax.broadcasted_iota(jnp.int32, (SSD_CHUNK, SSD_GW), 1) // SSD_HD

    a_es = [-jnp.exp(alog_ref[j]) * LOG2E for j in range(SSD_STREAMS)]
    expands = [(hrow == SSD_HPG * (pl.program_id(1) * SSD_STREAMS + j) + lcol // SSD_HD).astype(BF16)
               for j in range(SSD_STREAMS)]

    def stream_chunk(j, rows, h_t, halo):
        cw = cw_ref[j]
        a_e, expand = a_es[j], expands[j]

        dt = jnp.logaddexp(dt_ref[rows, :] + dtb, 0.0)
        dte = None
        for term in _split2(dt):
            part = jnp.dot(term, expand, preferred_element_type=F32)
            dte = part if dte is None else dte + part
        yield
        raw = jnp.concatenate([xs_ref[j, rows, :], bm_ref[j, rows, :], cm_ref[j, rows, :]], axis=1)
        shifted = jnp.dot(shift_ref[...], raw, preferred_element_type=F32)
        raw_f = raw.astype(F32)
        da = dte * a_e
        cs = None
        for term in _split2(da):
            part = jnp.dot(lmat, term, preferred_element_type=F32)
            cs = part if cs is None else cs + part
        zgate = zg_ref[j, rows, :].astype(F32)
        yield
        cs_t = cs.T
        cs_last = cs[SSD_CHUNK - 1:SSD_CHUNK, :]
        decays = []
        for k in range(SSD_HPG):
            colb = jnp.broadcast_to(cs[:, SSD_HD * k:SSD_HD * k + 1], (SSD_CHUNK, SSD_CHUNK))
            rowb = cs_t[SSD_HD * k:SSD_HD * k + 1, :]
            decays.append(jnp.exp2(jnp.where(tri, colb - rowb, NEG)))
        to_end = jnp.exp2(cs_last - cs)
        from_start = jnp.exp2(cs)
        yield
        hwin = jnp.concatenate([halo, jnp.zeros_like(halo)], axis=0)
        acc = cb_ref[j] + raw_f * cw[SSD_CONV - 1:SSD_CONV, :]
        edge = None
        for s in range(1, SSD_CONV):
            tap = cw[SSD_CONV - 1 - s:SSD_CONV - s, :]
            acc = acc + shifted[(s - 1) * SSD_CHUNK:s * SSD_CHUNK] * tap
            part = pltpu.roll(hwin, s, axis=0)[halo_rows:] * tap
            edge = part if edge is None else edge + part
        acc = jnp.concatenate([acc[:halo_rows] + edge, acc[halo_rows:]], axis=0)
        xbc = _silu(acc)
        xs = xbc[:, :SSD_GW]
        bm = xbc[:, SSD_GW:SSD_GW + SSD_STATE]
        cm = xbc[:, SSD_GW + SSD_STATE:]
        xdt = xs * dte
        xdt_b = xdt.astype(BF16)
        cm_b = cm.astype(BF16)
        cb = lax.dot_general(cm_b, bm.astype(BF16), _NT, preferred_element_type=F32)
        y_off = jnp.dot(cm_b, h_t.astype(BF16), preferred_element_type=F32)
        st = jnp.dot(bm.T.astype(BF16), (xdt * to_end).astype(BF16), preferred_element_type=F32)
        yield
        ws = [(cb * decays[k]).astype(BF16) for k in range(SSD_HPG)]
        xm = [jnp.where(head_of_lane == k, xdt_b, jnp.zeros_like(xdt_b)) for k in range(SSD_HPG)]
        y = jnp.dot(jnp.concatenate(ws, axis=1), jnp.concatenate(xm, axis=0),
                    preferred_element_type=F32)
        yield
        y = y + y_off * from_start + dskip_ref[j] * xs
        gated = y * zgate
        var = jnp.mean(gated * gated, axis=-1, keepdims=True)
        o_ref[j, rows, :] = (gated * lax.rsqrt(var + EPS) * nw_ref[j]).astype(BF16)
        return h_t * jnp.exp2(cs_last) + st, raw_f[SSD_CHUNK - halo_rows:, :]

    def chunk(c, carry):
        rows = pl.ds(pl.multiple_of(c * SSD_CHUNK, SSD_CHUNK), SSD_CHUNK)
        return _round_robin([stream_chunk(j, rows, *carry[j]) for j in range(SSD_STREAMS)])

    init = (jnp.zeros((SSD_STATE, SSD_GW), F32), jnp.zeros((halo_rows, conv_width), F32))
    lax.fori_loop(0, nchunks, chunk, (init,) * SSD_STREAMS)


def _ssd(xs, bm, cm, zg, dt, conv_w, conv_b, dt_bias, a_log, d_skip, norm_w, batch, t_len):
    m = batch * t_len
    g_blk = lambda w: pl.BlockSpec((SSD_STREAMS, t_len, w), lambda b, g: (g, b, 0))
    g_par = lambda r, w: pl.BlockSpec((SSD_STREAMS, r, w), lambda b, g: (g, 0, 0))
    bc_w = SSD_GROUPS * SSD_STATE

    def per_group(a):
        r = a.shape[0]
        return jnp.concatenate([a[:, :SSD_W].reshape(r, SSD_GROUPS, SSD_GW),
                                a[:, SSD_W:SSD_W + bc_w].reshape(r, SSD_GROUPS, SSD_STATE),
                                a[:, SSD_W + bc_w:].reshape(r, SSD_GROUPS, SSD_STATE)], axis=2).transpose(1, 0, 2)

    dtb = jnp.pad(dt_bias, (0, LANES - SSD_HEADS)).reshape(1, LANES)
    per_lane = lambda a: jnp.repeat(a, SSD_HD).reshape(SSD_GROUPS, 1, SSD_GW)
    conv_width = SSD_GW + 2 * SSD_STATE
    t_out = np.arange((SSD_CONV - 1) * SSD_CHUNK)[:, None]
    shift_table = jnp.asarray(t_out % SSD_CHUNK - (t_out // SSD_CHUNK + 1) == np.arange(SSD_CHUNK)[None, :], BF16)
    return pl.pallas_call(
        _ssd_kernel,
        grid=(batch, SSD_GROUPS // SSD_STREAMS),
        in_specs=[g_blk(SSD_GW), g_blk(SSD_STATE), g_blk(SSD_STATE), g_blk(SSD_GW),
                  pl.BlockSpec((t_len, LANES), lambda b, g: (b, 0)),
                  pl.BlockSpec(((SSD_CONV - 1) * SSD_CHUNK, SSD_CHUNK), lambda b, g: (0, 0)),
                  g_par(SSD_CONV, conv_width), g_par(1, conv_width),
                  pl.BlockSpec((1, LANES), lambda b, g: (0, 0)),
                  g_par(1, SSD_GW), g_par(1, SSD_GW), g_par(1, SSD_GW)],
        out_specs=g_blk(SSD_GW),
        out_shape=jax.ShapeDtypeStruct((SSD_GROUPS, m, SSD_GW), BF16),
        compiler_params=pltpu.CompilerParams(
            dimension_semantics=("arbitrary", "arbitrary"), vmem_limit_bytes=VMEM_LIMIT),
        name="ssd",
    )(xs, bm, cm, zg, dt, shift_table, per_group(conv_w), per_group(conv_b.reshape(1, SSD_XBC)), dtb,
      per_lane(a_log), per_lane(d_skip), norm_w.reshape(SSD_GROUPS, 1, SSD_GW))


def _mem_kernel(mem_ref, mnw_ref, wkv_ref, q_ref, g_ref, o_ref, *, q_chunk):
    t_len = q_ref.shape[1]
    mem = mem_ref[...]
    ms = jnp.mean(mem * mem, axis=-1, keepdims=True)
    mem_n = (mem * lax.rsqrt(ms + EPS) * mnw_ref[...]).astype(BF16)
    kv = jnp.dot(mem_n, wkv_ref[...], preferred_element_type=F32)
    k_all = (kv[:, :MEM_W] * (MEM_HD ** -0.5 * LOG2E)).astype(BF16)
    v_all = kv[:, MEM_W:].astype(BF16)
    def task(h, r0):
        k = k_all[:, h * MEM_HD:(h + 1) * MEM_HD]
        v = v_all[:, h * MEM_HD:(h + 1) * MEM_HD]
        s = lax.dot_general(q_ref[h, r0:r0 + q_chunk, :], k, _NT, preferred_element_type=F32)
        yield
        m = jnp.max(s, axis=-1, keepdims=True)
        p = jnp.exp2(s - m)
        l = jnp.sum(p, axis=-1, keepdims=True)
        yield
        o = jnp.dot(p.astype(BF16), v, preferred_element_type=F32) / l
        g = g_ref[h, r0:r0 + q_chunk, :].astype(F32)
        o_ref[h, r0:r0 + q_chunk, :] = (o * _silu(g)).astype(BF16)

    _pipeline([task(h, r0) for h in range(MEM_HEADS) for r0 in range(0, t_len, q_chunk)])


def _mem_attn(mem2d, mem_norm_w, w_kv, qm, gm, batch, t_len, mem_len):
    m = batch * t_len
    return pl.pallas_call(
        functools.partial(_mem_kernel, q_chunk=512),
        grid=(batch,),
        in_specs=[pl.BlockSpec((mem_len, D_MODEL), lambda b: (b, 0)),
                  pl.BlockSpec((1, D_MODEL), lambda b: (0, 0)),
                  pl.BlockSpec((D_MODEL, 2 * MEM_W), lambda b: (0, 0), pipeline_mode=pl.Buffered(1)),
                  pl.BlockSpec((MEM_HEADS, t_len, MEM_HD), lambda b: (0, b, 0)),
                  pl.BlockSpec((MEM_HEADS, t_len, MEM_HD), lambda b: (0, b, 0))],
        out_specs=pl.BlockSpec((MEM_HEADS, t_len, MEM_HD), lambda b: (0, b, 0)),
        out_shape=jax.ShapeDtypeStruct((MEM_HEADS, m, MEM_HD), BF16),
        compiler_params=pltpu.CompilerParams(
            dimension_semantics=("arbitrary",), vmem_limit_bytes=VMEM_LIMIT),
        name="mem_attn",
    )(mem2d, mem_norm_w, w_kv, qm, gm)


def _outproj_kernel(oatt_ref, ossd_ref, omem_ref, x_ref, w_ref, fnw_ref, o_ref):
    tm = x_ref.shape[0]
    n_sub = tm // PROJ_SUB
    fnw = fnw_ref[...]
    refs = ([(oatt_ref, i) for i in range(ATT_PAIRS)] + [(ossd_ref, i) for i in range(SSD_GROUPS)]
            + [(omem_ref, i) for i in range(MEM_HEADS)])
    h_prev = None
    for sb in range(n_sub + 1):
        if sb < n_sub:
            rows = slice(sb * PROJ_SUB, (sb + 1) * PROJ_SUB)
            mixed = jnp.concatenate([ref[i, rows, :] for ref, i in refs], axis=-1)
            h = x_ref[rows, :] + jnp.dot(mixed, w_ref[...], preferred_element_type=F32)
        if h_prev is not None:
            ms = jnp.mean(h_prev * h_prev, axis=-1, keepdims=True)
            o_ref[(sb - 1) * PROJ_SUB:sb * PROJ_SUB, :] = h_prev * lax.rsqrt(ms + EPS) * fnw
        h_prev = h


def _out_proj(oatt, ossd, omem, x2d, w_out, final_norm_w, tm):
    m = x2d.shape[0]
    lead = lambda i: (0, i, 0)
    return pl.pallas_call(
        _outproj_kernel,
        grid=(m // tm,),
        in_specs=[pl.BlockSpec((ATT_PAIRS, tm, LANES), lead),
                  pl.BlockSpec((SSD_GROUPS, tm, SSD_GW), lead),
                  pl.BlockSpec((MEM_HEADS, tm, MEM_HD), lead),
                  pl.BlockSpec((tm, D_MODEL), lambda i: (i, 0)),
                  pl.BlockSpec((MIX_W, D_MODEL), lambda i: (0, 0), pipeline_mode=pl.Buffered(1)),
                  pl.BlockSpec((1, D_MODEL), lambda i: (0, 0))],
        out_specs=pl.BlockSpec((tm, D_MODEL), lambda i: (i, 0)),
        out_shape=jax.ShapeDtypeStruct((m, D_MODEL), F32),
        compiler_params=pltpu.CompilerParams(
            dimension_semantics=("arbitrary",), vmem_limit_bytes=VMEM_LIMIT),
        name="out_proj",
    )(oatt, ossd, omem, x2d, w_out, final_norm_w)


def kernel(x, mem, norm_w, w_in, conv_w, conv_b, dt_bias, a_log, d_skip, ssd_norm_w,
           mem_norm_w, w_mem_kv, w_out, final_norm_w):
    batch, t_len, d_model = x.shape
    mem_len = mem.shape[1]
    depth = norm_w.shape[0]
    assert d_model == D_MODEL and depth == 1 and t_len % MOBA_BLOCK == 0 and t_len // MOBA_BLOCK == 8
    x2d = x.reshape(batch * t_len, D_MODEL)
    mem2d = mem.reshape(batch * mem_len, D_MODEL)
    w = w_in[0]
    w_bf = w.astype(BF16)
    w_b = w_bf[:, DT_COL0 + SSD_HEADS:]
    w_dt = jnp.pad(w_bf[:, DT_COL0:DT_COL0 + SSD_HEADS], ((0, 0), (0, LANES - SSD_HEADS)))

    qkvg, zg, xs, bm, cm, qm, gm, dt = _in_proj(x2d, norm_w[0].reshape(1, D_MODEL), w_bf, w_b, w_dt, tm=PROJ_TILE)
    o_att = _moba(qkvg, _moba_key_table(t_len), batch, t_len)
    o_ssd = _ssd(xs, bm, cm, zg, dt, conv_w[0], conv_b[0], dt_bias[0], a_log[0], d_skip[0],
                 ssd_norm_w[0], batch, t_len)
    o_mem = _mem_attn(mem2d, mem_norm_w[0].reshape(1, D_MODEL), w_mem_kv[0].astype(BF16), qm, gm,
                      batch, t_len, mem_len)
    out = _out_proj(o_att, o_ssd, o_mem, x2d, w_out[0].astype(BF16), final_norm_w.reshape(1, D_MODEL), tm=PROJ_TILE)
    return out.reshape(batch, t_len, D_MODEL)
```

```python
import functools

import jax
import jax.numpy as jnp
import numpy as np
from jax import lax
from jax.experimental import pallas as pl
from jax.experimental.pallas import tpu as pltpu

F32 = jnp.float32
BF16 = jnp.bfloat16

D_MODEL = 1024
ATT_HEADS = 8
ATT_HD = 64
ATT_W = ATT_HEADS * ATT_HD
ATT_PAIRS = ATT_HEADS // 2
MOBA_BLOCK = 256
MOBA_TOPK = 3
MOBA_PAIRS_PER_STEP = 2
SSD_W = 1024
SSD_HD = 64
SSD_HEADS = 16
SSD_GROUPS = 4
SSD_HPG = SSD_HEADS // SSD_GROUPS
SSD_GW = SSD_W // SSD_GROUPS
SSD_STREAMS = 4
SSD_STATE = 128
SSD_CONV = 4
SSD_CHUNK = 256
SSD_XBC = SSD_W + 2 * SSD_GROUPS * SSD_STATE
MEM_HEADS = 4
MEM_HD = 128
MEM_W = MEM_HEADS * MEM_HD
MIX_W = ATT_W + SSD_W + MEM_W
EPS = 1e-6
NEG = -1e30
LOG2E = 1.4426950408889634
_LOG2E_TERMS = (1.4453125, -0.00262451171875, 7.063150405883789e-06)
LANES = 128
PROJ_TILE = 1024
PROJ_SUB = 256
DT_COL0 = 4 * ATT_W + SSD_W + SSD_XBC
VMEM_LIMIT = 56 * 1024 * 1024

_NT = (((1,), (1,)), ((), ()))


def _split2(a):
    hi = a.astype(BF16)
    lo = (a - hi.astype(F32)).astype(BF16)
    return hi, lo


def _silu(a):
    return a * jax.nn.sigmoid(a)


def _pipeline(gens):
    live = []
    pending = list(gens)
    while pending or live:
        if pending:
            live.append(pending.pop(0))
        for g in list(reversed(live)):
            try:
                next(g)
            except StopIteration:
                live.remove(g)


def _round_robin(gens):
    results = [None] * len(gens)
    active = list(range(len(gens)))
    while active:
        for j in list(active):
            try:
                next(gens[j])
            except StopIteration as done:
                results[j] = done.value
                active.remove(j)
    return tuple(results)


def _inproj_kernel(x_ref, nw_ref, wa_ref, wb_ref, wdt_ref,
                   qkvg_ref, zg_ref, xs_ref, bm_ref, cm_ref, qm_ref, gm_ref, dt_ref):
    tm = x_ref.shape[0]
    n_sub = tm // PROJ_SUB
    nw = nw_ref[...]

    def normed(sb):
        x = x_ref[sb * PROJ_SUB:(sb + 1) * PROJ_SUB, :]
        ms = jnp.mean(x * x, axis=-1, keepdims=True)
        return (x * lax.rsqrt(ms + EPS) * nw).astype(BF16)

    u_next = normed(0)
    for sb in range(n_sub):
        u = u_next
        if sb + 1 < n_sub:
            u_next = normed(sb + 1)
        rows = slice(sb * PROJ_SUB, (sb + 1) * PROJ_SUB)

        def scatter(ref, w_ref, c0, piece, count, first=0, post=None):
            for c in range(0, piece * count, 512):
                r = jnp.dot(u, w_ref[:, c0 + c:c0 + c + 512], preferred_element_type=F32)
                if post is not None:
                    r = post(r)
                r = r.astype(BF16)
                for j in range(512 // piece):
                    ref[first + (c + j * piece) // piece, rows, :] = r[:, j * piece:(j + 1) * piece]

        scatter(qkvg_ref, wa_ref, 0, LANES, ATT_PAIRS, post=lambda r: r * (ATT_HD ** -0.5 * LOG2E))
        scatter(qkvg_ref, wa_ref, ATT_W, LANES, 3 * ATT_PAIRS, first=ATT_PAIRS)
        scatter(zg_ref, wa_ref, 4 * ATT_W, SSD_GW, SSD_GROUPS, post=_silu)
        scatter(xs_ref, wa_ref, 4 * ATT_W + SSD_W, SSD_GW, SSD_GROUPS)
        scatter(bm_ref, wa_ref, 4 * ATT_W + 2 * SSD_W, SSD_STATE, SSD_GROUPS)
        scatter(cm_ref, wa_ref, 4 * ATT_W + 2 * SSD_W + SSD_GROUPS * SSD_STATE, SSD_STATE, SSD_GROUPS)
        scatter(qm_ref, wb_ref, 0, MEM_HD, MEM_HEADS)
        scatter(gm_ref, wb_ref, MEM_W, MEM_HD, MEM_HEADS)
        dt_ref[rows, :] = jnp.dot(u, wdt_ref[...], preferred_element_type=F32)


def _in_proj(x2d, norm_w, w_a, w_b, w_dt, tm):
    m = x2d.shape[0]
    const = lambda i: (0, 0)
    lead = lambda i: (0, i, 0)
    out_shape = (
        jax.ShapeDtypeStruct((16, m, LANES), BF16),
        jax.ShapeDtypeStruct((SSD_GROUPS, m, SSD_GW), BF16),
        jax.ShapeDtypeStruct((SSD_GROUPS, m, SSD_GW), BF16),
        jax.ShapeDtypeStruct((SSD_GROUPS, m, SSD_STATE), BF16),
        jax.ShapeDtypeStruct((SSD_GROUPS, m, SSD_STATE), BF16),
        jax.ShapeDtypeStruct((MEM_HEADS, m, MEM_HD), BF16),
        jax.ShapeDtypeStruct((MEM_HEADS, m, MEM_HD), BF16),
        jax.ShapeDtypeStruct((m, LANES), F32),
    )
    out_specs = (
        pl.BlockSpec((16, tm, LANES), lead),
        pl.BlockSpec((SSD_GROUPS, tm, SSD_GW), lead),
        pl.BlockSpec((SSD_GROUPS, tm, SSD_GW), lead),
        pl.BlockSpec((SSD_GROUPS, tm, SSD_STATE), lead),
        pl.BlockSpec((SSD_GROUPS, tm, SSD_STATE), lead),
        pl.BlockSpec((MEM_HEADS, tm, MEM_HD), lead),
        pl.BlockSpec((MEM_HEADS, tm, MEM_HD), lead),
        pl.BlockSpec((tm, LANES), lambda i: (i, 0)),
    )
    return pl.pallas_call(
        _inproj_kernel,
        grid=(m // tm,),
        in_specs=[
            pl.BlockSpec((tm, D_MODEL), lambda i: (i, 0)),
            pl.BlockSpec((1, D_MODEL), const),
            pl.BlockSpec((D_MODEL, DT_COL0), const, pipeline_mode=pl.Buffered(1)),
            pl.BlockSpec((D_MODEL, 2 * MEM_W), const, pipeline_mode=pl.Buffered(1)),
            pl.BlockSpec((D_MODEL, LANES), const, pipeline_mode=pl.Buffered(1)),
        ],
        out_specs=out_specs,
        out_shape=out_shape,
        compiler_params=pltpu.CompilerParams(
            dimension_semantics=("arbitrary",), vmem_limit_bytes=VMEM_LIMIT),
        name="in_proj",
    )(x2d, norm_w, w_a, w_b, w_dt)


def _moba_pair_tasks(q_ref, k_ref, v_ref, g_ref, kc_ref, qc_ref, o_ref):
    t_len = q_ref.shape[0]
    nb = t_len // MOBA_BLOCK
    half = LANES // 2
    t_sel = min(MOBA_TOPK + 1, nb) * MOBA_BLOCK
    n_sel = t_len - t_sel

    k = k_ref[...]
    kc = kc_ref[...]
    lo = lax.broadcasted_iota(jnp.int32, (t_len, LANES), 1) < half
    ka = (jnp.where(lo, k, kc), jnp.where(lo, kc, k))
    lo_blk = lax.broadcasted_iota(jnp.int32, (MOBA_BLOCK, LANES), 1) < half
    key_i = lax.broadcasted_iota(jnp.int32, (MOBA_BLOCK, MOBA_BLOCK), 0)
    query_i = lax.broadcasted_iota(jnp.int32, (MOBA_BLOCK, MOBA_BLOCK), 1)
    causal_t = key_i <= query_i
    memo = {}

    def selection_lanes():
        if "aug" in memo:
            return memo["aug"]
        kmean = jnp.mean(k.astype(F32).reshape(nb, MOBA_BLOCK, LANES), axis=1)
        lo8 = lax.broadcasted_iota(jnp.int32, (nb, LANES), 1) < half
        terms = []
        for km in (jnp.where(lo8, kmean, 0.0), jnp.where(lo8, 0.0, kmean)):
            hi = km.astype(BF16).astype(F32)
            terms += [hi, km - hi]
        kstack = jnp.concatenate(terms, axis=0).astype(BF16)
        gate_t = lax.dot_general(kstack, q_ref[...], _NT, preferred_element_type=F32)[:, t_sel:]

        qblk = (lax.broadcasted_iota(jnp.int32, (nb, n_sel), 1) + t_sel) // MOBA_BLOCK
        jidx = lax.broadcasted_iota(jnp.int32, (nb, n_sel), 0)

        def select_bias(gate):
            cnt = jnp.zeros((nb, n_sel), jnp.int32)
            for i in range(nb):
                gi = gate[i:i + 1, :]
                beats = (gi > gate) | ((gi == gate) & (i < jidx))
                cnt = cnt + jnp.where(beats & (i < qblk), 1, 0)
            sel = ((cnt < MOBA_TOPK) & (jidx < qblk)) | (jidx == qblk)
            return jnp.where(sel, 0.0, NEG)

        bias0 = select_bias(gate_t[0:nb] + gate_t[nb:2 * nb])
        bias1 = select_bias(gate_t[2 * nb:3 * nb] + gate_t[3 * nb:4 * nb])
        crow = lax.broadcasted_iota(jnp.int32, (8, n_sel), 0)
        l2e = jnp.zeros((8, n_sel), F32)
        for i, term in enumerate(_LOG2E_TERMS + _LOG2E_TERMS):
            l2e = jnp.where(crow == i, term, l2e)
        pad = jnp.zeros((half - nb - 8, n_sel), F32)
        memo["aug"] = jnp.concatenate([bias1, l2e, pad, bias0, l2e, pad], axis=0).T.astype(BF16)
        return memo["aug"]

    def values_t():
        if "v" not in memo:
            v_t = v_ref[...].astype(F32).T.astype(BF16)
            ones_rows = (lax.broadcasted_iota(jnp.int32, (16, t_len), 0) == 0).astype(BF16)
            memo["v"] = [jnp.concatenate([v_t[h * ATT_HD:(h + 1) * ATT_HD], ones_rows], axis=0) for h in range(2)]
        return memo["v"]

    def head_task(n, h, sink):
        r0 = n * MOBA_BLOCK
        width = r0 + MOBA_BLOCK
        aug = qc_ref[...] if r0 < t_sel else selection_lanes()[r0 - t_sel:width - t_sel]
        q_blk = q_ref[r0:width, :]
        qa = jnp.where(lo_blk, q_blk, aug) if h == 0 else jnp.where(lo_blk, aug, q_blk)
        s = lax.dot_general(ka[h][:width], qa, _NT, preferred_element_type=F32)
        yield
        s_diag = jnp.where(causal_t, s[r0:], NEG)
        s = s_diag if n == 0 else jnp.concatenate([s[:r0], s_diag], axis=0)
        m = jnp.max(s, axis=0, keepdims=True)
        yield
        p = jnp.exp2(s - m).astype(BF16)
        yield
        o_t = jnp.dot(values_t()[h][:, :width], p, preferred_element_type=F32)
        sink[h] = o_t[:ATT_HD] / o_t[ATT_HD:ATT_HD + 1]
        if h == 1:
            o = jnp.concatenate([sink[0], sink[1]], axis=0).T
            g = g_ref[r0:width, :].astype(F32)
            o_ref[r0:width, :] = (o * _silu(g)).astype(BF16)

    sinks = [dict() for _ in range(nb)]
    return [head_task(n, h, sinks[n]) for n in range(nb) for h in range(2)]


def _moba_kernel(q_ref, k_ref, v_ref, g_ref, kc_ref, qc_ref, o_ref):
    per_pair = [_moba_pair_tasks(q_ref.at[i], k_ref.at[i], v_ref.at[i], g_ref.at[i], kc_ref.at[i], qc_ref, o_ref.at[i])
                for i in range(MOBA_PAIRS_PER_STEP)]
    tasks = []
    for t in range(0, len(per_pair[0]), 2):
        for pair in per_pair:
            tasks += pair[t:t + 2]
    _pipeline(tasks)


def _moba(qkvg, kc, batch, t_len):
    m = batch * t_len
    qc = np.zeros((MOBA_BLOCK, LANES), np.float32)
    for i, term in enumerate(_LOG2E_TERMS + _LOG2E_TERMS):
        qc[:, 8 + i] = term
        qc[:, LANES // 2 + 8 + i] = term
    qc = jnp.asarray(qc, BF16)
    per_step = MOBA_PAIRS_PER_STEP
    steps = ATT_PAIRS // per_step
    blk = lambda kind: pl.BlockSpec((per_step, t_len, LANES), lambda b, hp: (kind * steps + hp, b, 0))
    return pl.pallas_call(
        _moba_kernel,
        grid=(batch, steps),
        in_specs=[blk(0), blk(1), blk(2), blk(3),
                  pl.BlockSpec((per_step, t_len, LANES), lambda b, hp: (hp, 0, 0)),
                  pl.BlockSpec((MOBA_BLOCK, LANES), lambda b, hp: (0, 0))],
        out_specs=pl.BlockSpec((per_step, t_len, LANES), lambda b, hp: (hp, b, 0)),
        out_shape=jax.ShapeDtypeStruct((ATT_PAIRS, m, LANES), BF16),
        compiler_params=pltpu.CompilerParams(
            dimension_semantics=("arbitrary", "arbitrary"), vmem_limit_bytes=VMEM_LIMIT),
        name="moba",
    )(qkvg, qkvg, qkvg, qkvg, kc, qc)


def _moba_key_table(t_len):
    pos = np.arange(t_len)
    blk = pos // MOBA_BLOCK
    off = pos % MOBA_BLOCK
    slopes = np.exp2(-8.0 * np.arange(1, ATT_HEADS + 1, dtype=np.float32) / ATT_HEADS)
    onehot = (blk[:, None] == np.arange(8)[None, :]).astype(np.float32)
    half = LANES // 2

    def head_table(h):
        in_blk = (slopes[h] * off.astype(np.float32))[:, None]
        blk_start = (slopes[h] * MOBA_BLOCK * blk.astype(np.float32))[:, None]
        cols = [onehot, in_blk, in_blk, in_blk, blk_start, blk_start, blk_start,
                np.zeros((t_len, half - 14), np.float32)]
        return np.concatenate(cols, axis=1)

    tables = [np.concatenate([head_table(2 * hp + 1), head_table(2 * hp)], axis=1)
              for hp in range(ATT_PAIRS)]
    return jnp.asarray(np.stack(tables), BF16)


def _ssd_kernel(xs_ref, bm_ref, cm_ref, zg_ref, dt_ref, shift_ref, cw_ref, cb_ref,
                dtb_ref, alog_ref, dskip_ref, nw_ref, o_ref):
    t_len = xs_ref.shape[1]
    nchunks = t_len // SSD_CHUNK
    conv_width = SSD_GW + 2 * SSD_STATE
    halo_rows = 8

    dtb = dtb_ref[...]
    hrow = lax.broadcasted_iota(jnp.int32, (LANES, SSD_GW), 0)
    lcol = lax.broadcasted_iota(jnp.int32, (LANES, SSD_GW), 1)
    li = lax.broadcasted_iota(jnp.int32, (SSD_CHUNK, SSD_CHUNK), 0)
    si = lax.broadcasted_iota(jnp.int32, (SSD_CHUNK, SSD_CHUNK), 1)
    tri = li >= si
    lmat = tri.astype(BF16)
    head_of_lane = lax.broadcasted_iota(jnp.int32, (SSD_CHUNK, SSD_GW), 1) // SSD_HD

    a_es = [-jnp.exp(alog_ref[j]) * LOG2E for j in range(SSD_STREAMS)]
    expands = [(hrow == SSD_HPG * (pl.program_id(1) * SSD_STREAMS + j) + lcol // SSD_HD).astype(BF16)
               for j in range(SSD_STREAMS)]

    def stream_chunk(j, rows, h_t, halo):
        cw = cw_ref[j]
        a_e, expand = a_es[j], expands[j]

        dt = jnp.logaddexp(dt_ref[rows, :] + dtb, 0.0)
        dte = None
        for term in _split2(dt):
            part = jnp.dot(term, expand, preferred_element_type=F32)
            dte = part if dte is None else dte + part
        yield
        raw = jnp.concatenate([xs_ref[j, rows, :], bm_ref[j, rows, :], cm_ref[j, rows, :]], axis=1)
        shifted = jnp.dot(shift_ref[...], raw, preferred_element_type=F32)
        raw_f = raw.astype(F32)
        da = dte * a_e
        cs = None
        for term in _split2(da):
            part = jnp.dot(lmat, term, preferred_element_type=F32)
            cs = part if cs is None else cs + part
        zgate = zg_ref[j, rows, :].astype(F32)
        yield
        cs_t = cs.T
        cs_last = cs[SSD_CHUNK - 1:SSD_CHUNK, :]
        decays = []
        for k in range(SSD_HPG):
            colb = jnp.broadcast_to(cs[:, SSD_HD * k:SSD_HD * k + 1], (SSD_CHUNK, SSD_CHUNK))
            rowb = cs_t[SSD_HD * k:SSD_HD * k + 1, :]
            decays.append(jnp.exp2(jnp.where(tri, colb - rowb, NEG)))
        to_end = jnp.exp2(cs_last - cs)
        from_start = jnp.exp2(cs)
        yield
        hwin = jnp.concatenate([halo, jnp.zeros_like(halo)], axis=0)
        acc = cb_ref[j] + raw_f * cw[SSD_CONV - 1:SSD_CONV, :]
        edge = None
        for s in range(1, SSD_CONV):
            tap = cw[SSD_CONV - 1 - s:SSD_CONV - s, :]
            acc = acc + shifted[(s - 1) * SSD_CHUNK:s * SSD_CHUNK] * tap
            part = pltpu.roll(hwin, s, axis=0)[halo_rows:] * tap
            edge = part if edge is None else edge + part
        acc = jnp.concatenate([acc[:halo_rows] + edge, acc[halo_rows:]], axis=0)
        xbc = _silu(acc)
        xs = xbc[:, :SSD_GW]
        bm = xbc[:, SSD_GW:SSD_GW + SSD_STATE]
        cm = xbc[:, SSD_GW + SSD_STATE:]
        xdt = xs * dte
        xdt_b = xdt.astype(BF16)
        cm_b = cm.astype(BF16)
        cb = lax.dot_general(cm_b, bm.astype(BF16), _NT, preferred_element_type=F32)
        y_off = jnp.dot(cm_b, h_t.astype(BF16), preferred_element_type=F32)
        st = jnp.dot(bm.T.astype(BF16), (xdt * to_end).astype(BF16), preferred_element_type=F32)
        yield
        ws = [(cb * decays[k]).astype(BF16) for k in range(SSD_HPG)]
        xm = [jnp.where(head_of_lane == k, xdt_b, jnp.zeros_like(xdt_b)) for k in range(SSD_HPG)]
        y = jnp.dot(jnp.concatenate(ws, axis=1), jnp.concatenate(xm, axis=0),
                    preferred_element_type=F32)
        yield
        y = y + y_off * from_start + dskip_ref[j] * xs
        gated = y * zgate
        var = jnp.mean(gated * gated, axis=-1, keepdims=True)
        o_ref[j, rows, :] = (gated * lax.rsqrt(var + EPS) * nw_ref[j]).astype(BF16)
        return h_t * jnp.exp2(cs_last) + st, raw_f[SSD_CHUNK - halo_rows:, :]

    def chunk(c, carry):
        rows = pl.ds(pl.multiple_of(c * SSD_CHUNK, SSD_CHUNK), SSD_CHUNK)
        return _round_robin([stream_chunk(j, rows, *carry[j]) for j in range(SSD_STREAMS)])

    init = (jnp.zeros((SSD_STATE, SSD_GW), F32), jnp.zeros((halo_rows, conv_width), F32))
    lax.fori_loop(0, nchunks, chunk, (init,) * SSD_STREAMS)


def _ssd(xs, bm, cm, zg, dt, conv_w, conv_b, dt_bias, a_log, d_skip, norm_w, batch, t_len):
    m = batch * t_len
    g_blk = lambda w: pl.BlockSpec((SSD_STREAMS, t_len, w), lambda b, g: (g, b, 0))
    g_par = lambda r, w: pl.BlockSpec((SSD_STREAMS, r, w), lambda b, g: (g, 0, 0))
    bc_w = SSD_GROUPS * SSD_STATE

    def per_group(a):
        r = a.shape[0]
        return jnp.concatenate([a[:, :SSD_W].reshape(r, SSD_GROUPS, SSD_GW),
                                a[:, SSD_W:SSD_W + bc_w].reshape(r, SSD_GROUPS, SSD_STATE),
                                a[:, SSD_W + bc_w:].reshape(r, SSD_GROUPS, SSD_STATE)], axis=2).transpose(1, 0, 2)

    dtb = jnp.pad(dt_bias, (0, LANES - SSD_HEADS)).reshape(1, LANES)
    per_lane = lambda a: jnp.repeat(a, SSD_HD).reshape(SSD_GROUPS, 1, SSD_GW)
    conv_width = SSD_GW + 2 * SSD_STATE
    t_out = np.arange((SSD_CONV - 1) * SSD_CHUNK)[:, None]
    shift_table = jnp.asarray(t_out % SSD_CHUNK - (t_out // SSD_CHUNK + 1) == np.arange(SSD_CHUNK)[None, :], BF16)
    return pl.pallas_call(
        _ssd_kernel,
        grid=(batch, SSD_GROUPS // SSD_STREAMS),
        in_specs=[g_blk(SSD_GW), g_blk(SSD_STATE), g_blk(SSD_STATE), g_blk(SSD_GW),
                  pl.BlockSpec((t_len, LANES), lambda b, g: (b, 0)),
                  pl.BlockSpec(((SSD_CONV - 1) * SSD_CHUNK, SSD_CHUNK), lambda b, g: (0, 0)),
                  g_par(SSD_CONV, conv_width), g_par(1, conv_width),
                  pl.BlockSpec((1, LANES), lambda b, g: (0, 0)),
                  g_par(1, SSD_GW), g_par(1, SSD_GW), g_par(1, SSD_GW)],
        out_specs=g_blk(SSD_GW),
        out_shape=jax.ShapeDtypeStruct((SSD_GROUPS, m, SSD_GW), BF16),
        compiler_params=pltpu.CompilerParams(
            dimension_semantics=("arbitrary", "arbitrary"), vmem_limit_bytes=VMEM_LIMIT),
        name="ssd",
    )(xs, bm, cm, zg, dt, shift_table, per_group(conv_w), per_group(conv_b.reshape(1, SSD_XBC)), dtb,
      per_lane(a_log), per_lane(d_skip), norm_w.reshape(SSD_GROUPS, 1, SSD_GW))


def _mem_kernel(mem_ref, mnw_ref, wkv_ref, q_ref, g_ref, o_ref, *, q_chunk):
    t_len = q_ref.shape[1]
    mem = mem_ref[...]
    ms = jnp.mean(mem * mem, axis=-1, keepdims=True)
    mem_n = (mem * lax.rsqrt(ms + EPS) * mnw_ref[...]).astype(BF16)
    kv = jnp.dot(mem_n, wkv_ref[...], preferred_element_type=F32)
    k_all = (kv[:, :MEM_W] * (MEM_HD ** -0.5 * LOG2E)).astype(BF16)
    v_all = kv[:, MEM_W:].astype(BF16)
    def task(h, r0):
        k = k_all[:, h * MEM_HD:(h + 1) * MEM_HD]
        v = v_all[:, h * MEM_HD:(h + 1) * MEM_HD]
        s = lax.dot_general(q_ref[h, r0:r0 + q_chunk, :], k, _NT, preferred_element_type=F32)
        yield
        m = jnp.max(s, axis=-1, keepdims=True)
        p = jnp.exp2(s - m)
        l = jnp.sum(p, axis=-1, keepdims=True)
        yield
        o = jnp.dot(p.astype(BF16), v, preferred_element_type=F32) / l
        g = g_ref[h, r0:r0 + q_chunk, :].astype(F32)
        o_ref[h, r0:r0 + q_chunk, :] = (o * _silu(g)).astype(BF16)

    _pipeline([task(h, r0) for h in range(MEM_HEADS) for r0 in range(0, t_len, q_chunk)])


def _mem_attn(mem2d, mem_norm_w, w_kv, qm, gm, batch, t_len, mem_len):
    m = batch * t_len
    return pl.pallas_call(
        functools.partial(_mem_kernel, q_chunk=512),
        grid=(batch,),
        in_specs=[pl.BlockSpec((mem_len, D_MODEL), lambda b: (b, 0)),
                  pl.BlockSpec((1, D_MODEL), lambda b: (0, 0)),
                  pl.BlockSpec((D_MODEL, 2 * MEM_W), lambda b: (0, 0), pipeline_mode=pl.Buffered(1)),
                  pl.BlockSpec((MEM_HEADS, t_len, MEM_HD), lambda b: (0, b, 0)),
                  pl.BlockSpec((MEM_HEADS, t_len, MEM_HD), lambda b: (0, b, 0))],
        out_specs=pl.BlockSpec((MEM_HEADS, t_len, MEM_HD), lambda b: (0, b, 0)),
        out_shape=jax.ShapeDtypeStruct((MEM_HEADS, m, MEM_HD), BF16),
        compiler_params=pltpu.CompilerParams(
            dimension_semantics=("arbitrary",), vmem_limit_bytes=VMEM_LIMIT),
        name="mem_attn",
    )(mem2d, mem_norm_w, w_kv, qm, gm)


def _outproj_kernel(oatt_ref, ossd_ref, omem_ref, x_ref, w_ref, fnw_ref, o_ref):
    tm = x_ref.shape[0]
    n_sub = tm // PROJ_SUB
    fnw = fnw_ref[...]
    refs = ([(oatt_ref, i) for i in range(ATT_PAIRS)] + [(ossd_ref, i) for i in range(SSD_GROUPS)]
            + [(omem_ref, i) for i in range(MEM_HEADS)])
    h_prev = None
    for sb in range(n_sub + 1):
        if sb < n_sub:
            rows = slice(sb * PROJ_SUB, (sb + 1) * PROJ_SUB)
            mixed = jnp.concatenate([ref[i, rows, :] for ref, i in refs], axis=-1)
            h = x_ref[rows, :] + jnp.dot(mixed, w_ref[...], preferred_element_type=F32)
        if h_prev is not None:
            ms = jnp.mean(h_prev * h_prev, axis=-1, keepdims=True)
            o_ref[(sb - 1) * PROJ_SUB:sb * PROJ_SUB, :] = h_prev * lax.rsqrt(ms + EPS) * fnw
        h_prev = h


def _out_proj(oatt, ossd, omem, x2d, w_out, final_norm_w, tm):
    m = x2d.shape[0]
    lead = lambda i: (0, i, 0)
    return pl.pallas_call(
        _outproj_kernel,
        grid=(m // tm,),
        in_specs=[pl.BlockSpec((ATT_PAIRS, tm, LANES), lead),
                  pl.BlockSpec((SSD_GROUPS, tm, SSD_GW), lead),
                  pl.BlockSpec((MEM_HEADS, tm, MEM_HD), lead),
                  pl.BlockSpec((tm, D_MODEL), lambda i: (i, 0)),
                  pl.BlockSpec((MIX_W, D_MODEL), lambda i: (0, 0), pipeline_mode=pl.Buffered(1)),
                  pl.BlockSpec((1, D_MODEL), lambda i: (0, 0))],
        out_specs=pl.BlockSpec((tm, D_MODEL), lambda i: (i, 0)),
        out_shape=jax.ShapeDtypeStruct((m, D_MODEL), F32),
        compiler_params=pltpu.CompilerParams(
            dimension_semantics=("arbitrary",), vmem_limit_bytes=VMEM_LIMIT),
        name="out_proj",
    )(oatt, ossd, omem, x2d, w_out, final_norm_w)


def kernel(x, mem, norm_w, w_in, conv_w, conv_b, dt_bias, a_log, d_skip, ssd_norm_w,
           mem_norm_w, w_mem_kv, w_out, final_norm_w):
    batch, t_len, d_model = x.shape
    mem_len = mem.shape[1]
    depth = norm_w.shape[0]
    assert d_model == D_MODEL and depth == 1 and t_len % MOBA_BLOCK == 0 and t_len // MOBA_BLOCK == 8
    x2d = x.reshape(batch * t_len, D_MODEL)
    mem2d = mem.reshape(batch * mem_len, D_MODEL)
    w = w_in[0]
    w_bf = w.astype(BF16)
    w_b = w_bf[:, DT_COL0 + SSD_HEADS:]
    w_dt = jnp.pad(w_bf[:, DT_COL0:DT_COL0 + SSD_HEADS], ((0, 0), (0, LANES - SSD_HEADS)))

    qkvg, zg, xs, bm, cm, qm, gm, dt = _in_proj(x2d, norm_w[0].reshape(1, D_MODEL), w_bf, w_b, w_dt, tm=PROJ_TILE)
    o_att = _moba(qkvg, _moba_key_table(t_len), batch, t_len)
    o_ssd = _ssd(xs, bm, cm, zg, dt, conv_w[0], conv_b[0], dt_bias[0], a_log[0], d_skip[0],
                 ssd_norm_w[0], batch, t_len)
    o_mem = _mem_attn(mem2d, mem_norm_w[0].reshape(1, D_MODEL), w_mem_kv[0].astype(BF16), qm, gm,
                      batch, t_len, mem_len)
    out = _out_proj(o_att, o_ssd, o_mem, x2d, w_out[0].astype(BF16), final_norm_w.reshape(1, D_MODEL), tm=PROJ_TILE)
    return out.reshape(batch, t_len, D_MODEL)
```

```python
import functools

import jax
import jax.numpy as jnp
import numpy as np
from jax import lax
from jax.experimental import pallas as pl
from jax.experimental.pallas import tpu as pltpu

F32 = jnp.float32
BF16 = jnp.bfloat16

D_MODEL = 1024
ATT_HEADS = 8
ATT_HD = 64
ATT_W = ATT_HEADS * ATT_HD
ATT_PAIRS = ATT_HEADS // 2
MOBA_BLOCK = 256
MOBA_TOPK = 3
MOBA_PAIRS_PER_STEP = 2
SSD_W = 1024
SSD_HD = 64
SSD_HEADS = 16
SSD_GROUPS = 4
SSD_HPG = SSD_HEADS // SSD_GROUPS
SSD_GW = SSD_W // SSD_GROUPS
SSD_STREAMS = 4
SSD_STATE = 128
SSD_CONV = 4
SSD_CHUNK = 256
SSD_XBC = SSD_W + 2 * SSD_GROUPS * SSD_STATE
MEM_HEADS = 4
MEM_HD = 128
MEM_W = MEM_HEADS * MEM_HD
MIX_W = ATT_W + SSD_W + MEM_W
EPS = 1e-6
NEG = -1e30
LOG2E = 1.4426950408889634
_LOG2E_TERMS = (1.4453125, -0.00262451171875, 7.063150405883789e-06)
LANES = 128
PROJ_TILE = 1024
PROJ_SUB = 256
DT_COL0 = 4 * ATT_W + SSD_W + SSD_XBC
VMEM_LIMIT = 56 * 1024 * 1024

_NT = (((1,), (1,)), ((), ()))


def _split2(a):
    hi = a.astype(BF16)
    lo = (a - hi.astype(F32)).astype(BF16)
    return hi, lo


def _silu(a):
    return a * jax.nn.sigmoid(a)


def _pipeline(gens):
    live = []
    pending = list(gens)
    while pending or live:
        if pending:
            live.append(pending.pop(0))
        for g in list(reversed(live)):
            try:
                next(g)
            except StopIteration:
                live.remove(g)


def _round_robin(gens):
    results = [None] * len(gens)
    active = list(range(len(gens)))
    while active:
        for j in list(active):
            try:
                next(gens[j])
            except StopIteration as done:
                results[j] = done.value
                active.remove(j)
    return tuple(results)


def _inproj_kernel(x_ref, nw_ref, wa_ref, wb_ref, wdt_ref,
                   qkvg_ref, zg_ref, xs_ref, bm_ref, cm_ref, qm_ref, gm_ref, dt_ref):
    tm = x_ref.shape[0]
    n_sub = tm // PROJ_SUB
    nw = nw_ref[...]

    def normed(sb):
        x = x_ref[sb * PROJ_SUB:(sb + 1) * PROJ_SUB, :]
        ms = jnp.mean(x * x, axis=-1, keepdims=True)
        return (x * lax.rsqrt(ms + EPS) * nw).astype(BF16)

    u_next = normed(0)
    for sb in range(n_sub):
        u = u_next
        if sb + 1 < n_sub:
            u_next = normed(sb + 1)
        rows = slice(sb * PROJ_SUB, (sb + 1) * PROJ_SUB)

        def scatter(ref, w_ref, c0, piece, count, first=0, post=None):
            for c in range(0, piece * count, 512):
                r = jnp.dot(u, w_ref[:, c0 + c:c0 + c + 512], preferred_element_type=F32)
                if post is not None:
                    r = post(r)
                r = r.astype(BF16)
                for j in range(512 // piece):
                    ref[first + (c + j * piece) // piece, rows, :] = r[:, j * piece:(j + 1) * piece]

        scatter(qkvg_ref, wa_ref, 0, LANES, ATT_PAIRS, post=lambda r: r * (ATT_HD ** -0.5 * LOG2E))
        scatter(qkvg_ref, wa_ref, ATT_W, LANES, 2 * ATT_PAIRS, first=ATT_PAIRS)
        scatter(qkvg_ref, wa_ref, 3 * ATT_W, LANES, ATT_PAIRS, first=3 * ATT_PAIRS, post=_silu)
        scatter(zg_ref, wa_ref, 4 * ATT_W, SSD_GW, SSD_GROUPS, post=_silu)
        scatter(xs_ref, wa_ref, 4 * ATT_W + SSD_W, SSD_GW, SSD_GROUPS)
        scatter(bm_ref, wa_ref, 4 * ATT_W + 2 * SSD_W, SSD_STATE, SSD_GROUPS)
        scatter(cm_ref, wa_ref, 4 * ATT_W + 2 * SSD_W + SSD_GROUPS * SSD_STATE, SSD_STATE, SSD_GROUPS)
        scatter(qm_ref, wb_ref, 0, MEM_HD, MEM_HEADS)
        scatter(gm_ref, wb_ref, MEM_W, MEM_HD, MEM_HEADS, post=_silu)
        dt_ref[rows, :] = jnp.dot(u, wdt_ref[...], preferred_element_type=F32)


def _in_proj(x2d, norm_w, w_a, w_b, w_dt, tm):
    m = x2d.shape[0]
    const = lambda i: (0, 0)
    lead = lambda i: (0, i, 0)
    out_shape = (
        jax.ShapeDtypeStruct((16, m, LANES), BF16),
        jax.ShapeDtypeStruct((SSD_GROUPS, m, SSD_GW), BF16),
        jax.ShapeDtypeStruct((SSD_GROUPS, m, SSD_GW), BF16),
        jax.ShapeDtypeStruct((SSD_GROUPS, m, SSD_STATE), BF16),
        jax.ShapeDtypeStruct((SSD_GROUPS, m, SSD_STATE), BF16),
        jax.ShapeDtypeStruct((MEM_HEADS, m, MEM_HD), BF16),
        jax.ShapeDtypeStruct((MEM_HEADS, m, MEM_HD), BF16),
        jax.ShapeDtypeStruct((m, LANES), F32),
    )
    out_specs = (
        pl.BlockSpec((16, tm, LANES), lead),
        pl.BlockSpec((SSD_GROUPS, tm, SSD_GW), lead),
        pl.BlockSpec((SSD_GROUPS, tm, SSD_GW), lead),
        pl.BlockSpec((SSD_GROUPS, tm, SSD_STATE), lead),
        pl.BlockSpec((SSD_GROUPS, tm, SSD_STATE), lead),
        pl.BlockSpec((MEM_HEADS, tm, MEM_HD), lead),
        pl.BlockSpec((MEM_HEADS, tm, MEM_HD), lead),
        pl.BlockSpec((tm, LANES), lambda i: (i, 0)),
    )
    return pl.pallas_call(
        _inproj_kernel,
        grid=(m // tm,),
        in_specs=[
            pl.BlockSpec((tm, D_MODEL), lambda i: (i, 0)),
            pl.BlockSpec((1, D_MODEL), const),
            pl.BlockSpec((D_MODEL, DT_COL0), const, pipeline_mode=pl.Buffered(1)),
            pl.BlockSpec((D_MODEL, 2 * MEM_W), const, pipeline_mode=pl.Buffered(1)),
            pl.BlockSpec((D_MODEL, LANES), const, pipeline_mode=pl.Buffered(1)),
        ],
        out_specs=out_specs,
        out_shape=out_shape,
        compiler_params=pltpu.CompilerParams(
            dimension_semantics=("arbitrary",), vmem_limit_bytes=VMEM_LIMIT),
        name="in_proj",
    )(x2d, norm_w, w_a, w_b, w_dt)


def _moba_pair_tasks(q_ref, k_ref, v_ref, g_ref, kc_ref, qc_ref, o_ref):
    t_len = q_ref.shape[0]
    nb = t_len // MOBA_BLOCK
    half = LANES // 2
    t_sel = min(MOBA_TOPK + 1, nb) * MOBA_BLOCK
    n_sel = t_len - t_sel

    k = k_ref[...]
    kc = kc_ref[...]
    lo = lax.broadcasted_iota(jnp.int32, (t_len, LANES), 1) < half
    ka = (jnp.where(lo, k, kc), jnp.where(lo, kc, k))
    lo_blk = lax.broadcasted_iota(jnp.int32, (MOBA_BLOCK, LANES), 1) < half
    key_i = lax.broadcasted_iota(jnp.int32, (MOBA_BLOCK, MOBA_BLOCK), 0)
    query_i = lax.broadcasted_iota(jnp.int32, (MOBA_BLOCK, MOBA_BLOCK), 1)
    causal_t = key_i <= query_i
    memo = {}

    def selection_lanes():
        if "aug" in memo:
            return memo["aug"]
        kmean = jnp.mean(k.astype(F32).reshape(nb, MOBA_BLOCK, LANES), axis=1)
        lo8 = lax.broadcasted_iota(jnp.int32, (nb, LANES), 1) < half
        terms = []
        for km in (jnp.where(lo8, kmean, 0.0), jnp.where(lo8, 0.0, kmean)):
            hi = km.astype(BF16).astype(F32)
            terms += [hi, km - hi]
        kstack = jnp.concatenate(terms, axis=0).astype(BF16)
        gate_t = lax.dot_general(kstack, q_ref[...], _NT, preferred_element_type=F32)[:, t_sel:]

        qblk = (lax.broadcasted_iota(jnp.int32, (nb, n_sel), 1) + t_sel) // MOBA_BLOCK
        jidx = lax.broadcasted_iota(jnp.int32, (nb, n_sel), 0)

        def select_bias(gate):
            cnt = jnp.zeros((nb, n_sel), jnp.int32)
            for i in range(nb):
                gi = gate[i:i + 1, :]
                beats = (gi > gate) | ((gi == gate) & (i < jidx))
                cnt = cnt + jnp.where(beats & (i < qblk), 1, 0)
            sel = ((cnt < MOBA_TOPK) & (jidx < qblk)) | (jidx == qblk)
            return jnp.where(sel, 0.0, NEG)

        bias0 = select_bias(gate_t[0:nb] + gate_t[nb:2 * nb])
        bias1 = select_bias(gate_t[2 * nb:3 * nb] + gate_t[3 * nb:4 * nb])
        crow = lax.broadcasted_iota(jnp.int32, (8, n_sel), 0)
        l2e = jnp.zeros((8, n_sel), F32)
        for i, term in enumerate(_LOG2E_TERMS + _LOG2E_TERMS):
            l2e = jnp.where(crow == i, term, l2e)
        pad = jnp.zeros((half - nb - 8, n_sel), F32)
        memo["aug"] = jnp.concatenate([bias1, l2e, pad, bias0, l2e, pad], axis=0).T.astype(BF16)
        return memo["aug"]

    def values_t():
        if "v" not in memo:
            v_t = v_ref[...].astype(F32).T.astype(BF16)
            ones_rows = (lax.broadcasted_iota(jnp.int32, (16, t_len), 0) == 0).astype(BF16)
            memo["v"] = [jnp.concatenate([v_t[h * ATT_HD:(h + 1) * ATT_HD], ones_rows], axis=0) for h in range(2)]
        return memo["v"]

    def head_task(n, h, sink):
        r0 = n * MOBA_BLOCK
        width = r0 + MOBA_BLOCK
        aug = qc_ref[...] if r0 < t_sel else selection_lanes()[r0 - t_sel:width - t_sel]
        q_blk = q_ref[r0:width, :]
        qa = jnp.where(lo_blk, q_blk, aug) if h == 0 else jnp.where(lo_blk, aug, q_blk)
        s = lax.dot_general(ka[h][:width], qa, _NT, preferred_element_type=F32)
        yield
        s_diag = jnp.where(causal_t, s[r0:], NEG)
        s = s_diag if n == 0 else jnp.concatenate([s[:r0], s_diag], axis=0)
        m = jnp.max(s, axis=0, keepdims=True)
        yield
        p = jnp.exp2(s - m).astype(BF16)
        yield
        o_t = jnp.dot(values_t()[h][:, :width], p, preferred_element_type=F32)
        sink[h] = o_t[:ATT_HD] / o_t[ATT_HD:ATT_HD + 1]
        if h == 1:
            o = jnp.concatenate([sink[0], sink[1]], axis=0).T
            g = g_ref[r0:width, :].astype(F32)
            o_ref[r0:width, :] = (o * g).astype(BF16)

    sinks = [dict() for _ in range(nb)]
    return [head_task(n, h, sinks[n]) for n in range(nb) for h in range(2)]


def _moba_kernel(q_ref, k_ref, v_ref, g_ref, kc_ref, qc_ref, o_ref):
    per_pair = [_moba_pair_tasks(q_ref.at[i], k_ref.at[i], v_ref.at[i], g_ref.at[i], kc_ref.at[i], qc_ref, o_ref.at[i])
                for i in range(MOBA_PAIRS_PER_STEP)]
    tasks = []
    for t in range(0, len(per_pair[0]), 2):
        for pair in per_pair:
            tasks += pair[t:t + 2]
    _pipeline(tasks)


def _moba(qkvg, kc, batch, t_len):
    m = batch * t_len
    qc = np.zeros((MOBA_BLOCK, LANES), np.float32)
    for i, term in enumerate(_LOG2E_TERMS + _LOG2E_TERMS):
        qc[:, 8 + i] = term
        qc[:, LANES // 2 + 8 + i] = term
    qc = jnp.asarray(qc, BF16)
    per_step = MOBA_PAIRS_PER_STEP
    steps = ATT_PAIRS // per_step
    blk = lambda kind: pl.BlockSpec((per_step, t_len, LANES), lambda b, hp: (kind * steps + hp, b, 0))
    return pl.pallas_call(
        _moba_kernel,
        grid=(batch, steps),
        in_specs=[blk(0), blk(1), blk(2), blk(3),
                  pl.BlockSpec((per_step, t_len, LANES), lambda b, hp: (hp, 0, 0)),
                  pl.BlockSpec((MOBA_BLOCK, LANES), lambda b, hp: (0, 0))],
        out_specs=pl.BlockSpec((per_step, t_len, LANES), lambda b, hp: (hp, b, 0)),
        out_shape=jax.ShapeDtypeStruct((ATT_PAIRS, m, LANES), BF16),
        compiler_params=pltpu.CompilerParams(
            dimension_semantics=("arbitrary", "arbitrary"), vmem_limit_bytes=VMEM_LIMIT),
        name="moba",
    )(qkvg, qkvg, qkvg, qkvg, kc, qc)


def _moba_key_table(t_len):
    pos = np.arange(t_len)
    blk = pos // MOBA_BLOCK
    off = pos % MOBA_BLOCK
    slopes = np.exp2(-8.0 * np.arange(1, ATT_HEADS + 1, dtype=np.float32) / ATT_HEADS)
    onehot = (blk[:, None] == np.arange(8)[None, :]).astype(np.float32)
    half = LANES // 2

    def head_table(h):
        in_blk = (slopes[h] * off.astype(np.float32))[:, None]
        blk_start = (slopes[h] * MOBA_BLOCK * blk.astype(np.float32))[:, None]
        cols = [onehot, in_blk, in_blk, in_blk, blk_start, blk_start, blk_start,
                np.zeros((t_len, half - 14), np.float32)]
        return np.concatenate(cols, axis=1)

    tables = [np.concatenate([head_table(2 * hp + 1), head_table(2 * hp)], axis=1)
              for hp in range(ATT_PAIRS)]
    return jnp.asarray(np.stack(tables), BF16)


def _ssd_kernel(xs_ref, bm_ref, cm_ref, zg_ref, dt_ref, shift_ref, cw_ref, cb_ref,
                dtb_ref, alog_ref, dskip_ref, nw_ref, o_ref):
    t_len = xs_ref.shape[1]
    nchunks = t_len // SSD_CHUNK
    conv_width = SSD_GW + 2 * SSD_STATE
    halo_rows = 8

    dtb = dtb_ref[...]
    hrow = lax.broadcasted_iota(jnp.int32, (LANES, SSD_GW), 0)
    lcol = lax.broadcasted_iota(jnp.int32, (LANES, SSD_GW), 1)
    li = lax.broadcasted_iota(jnp.int32, (SSD_CHUNK, SSD_CHUNK), 0)
    si = lax.broadcasted_iota(jnp.int32, (SSD_CHUNK, SSD_CHUNK), 1)
    tri = li >= si
    lmat = tri.astype(BF16)
    head_of_lane = lax.broadcasted_iota(jnp.int32, (SSD_CHUNK, SSD_GW), 1) // SSD_HD

    a_es = [-jnp.exp(alog_ref[j]) * LOG2E for j in range(SSD_STREAMS)]
    expands = [(hrow == SSD_HPG * (pl.program_id(1) * SSD_STREAMS + j) + lcol // SSD_HD).astype(BF16)
               for j in range(SSD_STREAMS)]

    def stream_chunk(j, rows, h_t, halo):
        cw = cw_ref[j]
        a_e, expand = a_es[j], expands[j]

        dt = jnp.logaddexp(dt_ref[rows, :] + dtb, 0.0)
        dte = None
        for term in _split2(dt):
            part = jnp.dot(term, expand, preferred_element_type=F32)
            dte = part if dte is None else dte + part
        yield
        raw = jnp.concatenate([xs_ref[j, rows, :], bm_ref[j, rows, :], cm_ref[j, rows, :]], axis=1)
        shifted = jnp.dot(shift_ref[...], raw, preferred_element_type=F32)
        raw_f = raw.astype(F32)
        da = dte * a_e
        cs = None
        for term in _split2(da):
            part = jnp.dot(lmat, term, preferred_element_type=F32)
            cs = part if cs is None else cs + part
        zgate = zg_ref[j, rows, :].astype(F32)
        yield
        cs_t = cs.T
        cs_last = cs[SSD_CHUNK - 1:SSD_CHUNK, :]
        to_end = jnp.exp2(cs_last - cs)
        from_start = jnp.exp2(cs)
        yield
        hwin = jnp.concatenate([halo, jnp.zeros_like(halo)], axis=0)
        acc = cb_ref[j] + raw_f * cw[SSD_CONV - 1:SSD_CONV, :]
        edge = None
        for s in range(1, SSD_CONV):
            tap = cw[SSD_CONV - 1 - s:SSD_CONV - s, :]
            acc = acc + shifted[(s - 1) * SSD_CHUNK:s * SSD_CHUNK] * tap
            part = pltpu.roll(hwin, s, axis=0)[halo_rows:] * tap
            edge = part if edge is None else edge + part
        acc = jnp.concatenate([acc[:halo_rows] + edge, acc[halo_rows:]], axis=0)
        xbc = _silu(acc)
        xs = xbc[:, :SSD_GW]
        bm = xbc[:, SSD_GW:SSD_GW + SSD_STATE]
        cm = xbc[:, SSD_GW + SSD_STATE:]
        xdt = xs * dte
        xdt_b = xdt.astype(BF16)
        cm_b = cm.astype(BF16)
        cb = lax.dot_general(cm_b, bm.astype(BF16), _NT, preferred_element_type=F32)
        y_off = jnp.dot(cm_b, h_t.astype(BF16), preferred_element_type=F32)
        st = jnp.dot(bm.T.astype(BF16), (xdt * to_end).astype(BF16), preferred_element_type=F32)
        yield
        ws = []
        for k in range(SSD_HPG):
            colb = jnp.broadcast_to(cs[:, SSD_HD * k:SSD_HD * k + 1], (SSD_CHUNK, SSD_CHUNK))
            rowb = cs_t[SSD_HD * k:SSD_HD * k + 1, :]
            ws.append((cb * jnp.exp2(jnp.where(tri, colb - rowb, NEG))).astype(BF16))
        xm = [jnp.where(head_of_lane == k, xdt_b, jnp.zeros_like(xdt_b)) for k in range(SSD_HPG)]
        y = jnp.dot(jnp.concatenate(ws, axis=1), jnp.concatenate(xm, axis=0),
                    preferred_element_type=F32)
        yield
        y = y + y_off * from_start + dskip_ref[j] * xs
        gated = y * zgate
        var = jnp.mean(gated * gated, axis=-1, keepdims=True)
        o_ref[j, rows, :] = (gated * lax.rsqrt(var + EPS) * nw_ref[j]).astype(BF16)
        return h_t * jnp.exp2(cs_last) + st, raw_f[SSD_CHUNK - halo_rows:, :]

    def chunk(c, carry):
        rows = pl.ds(pl.multiple_of(c * SSD_CHUNK, SSD_CHUNK), SSD_CHUNK)
        return _round_robin([stream_chunk(j, rows, *carry[j]) for j in range(SSD_STREAMS)])

    init = (jnp.zeros((SSD_STATE, SSD_GW), F32), jnp.zeros((halo_rows, conv_width), F32))
    lax.fori_loop(0, nchunks, chunk, (init,) * SSD_STREAMS)


def _ssd(xs, bm, cm, zg, dt, conv_w, conv_b, dt_bias, a_log, d_skip, norm_w, batch, t_len):
    m = batch * t_len
    g_blk = lambda w: pl.BlockSpec((SSD_STREAMS, t_len, w), lambda b, g: (g, b, 0))
    g_par = lambda r, w: pl.BlockSpec((SSD_STREAMS, r, w), lambda b, g: (g, 0, 0))
    bc_w = SSD_GROUPS * SSD_STATE

    def per_group(a):
        r = a.shape[0]
        return jnp.concatenate([a[:, :SSD_W].reshape(r, SSD_GROUPS, SSD_GW),
                                a[:, SSD_W:SSD_W + bc_w].reshape(r, SSD_GROUPS, SSD_STATE),
                                a[:, SSD_W + bc_w:].reshape(r, SSD_GROUPS, SSD_STATE)], axis=2).transpose(1, 0, 2)

    dtb = jnp.pad(dt_bias, (0, LANES - SSD_HEADS)).reshape(1, LANES)
    per_lane = lambda a: jnp.repeat(a, SSD_HD).reshape(SSD_GROUPS, 1, SSD_GW)
    conv_width = SSD_GW + 2 * SSD_STATE
    t_out = np.arange((SSD_CONV - 1) * SSD_CHUNK)[:, None]
    shift_table = jnp.asarray(t_out % SSD_CHUNK - (t_out // SSD_CHUNK + 1) == np.arange(SSD_CHUNK)[None, :], BF16)
    return pl.pallas_call(
        _ssd_kernel,
        grid=(batch, SSD_GROUPS // SSD_STREAMS),
        in_specs=[g_blk(SSD_GW), g_blk(SSD_STATE), g_blk(SSD_STATE), g_blk(SSD_GW),
                  pl.BlockSpec((t_len, LANES), lambda b, g: (b, 0)),
                  pl.BlockSpec(((SSD_CONV - 1) * SSD_CHUNK, SSD_CHUNK), lambda b, g: (0, 0)),
                  g_par(SSD_CONV, conv_width), g_par(1, conv_width),
                  pl.BlockSpec((1, LANES), lambda b, g: (0, 0)),
                  g_par(1, SSD_GW), g_par(1, SSD_GW), g_par(1, SSD_GW)],
        out_specs=g_blk(SSD_GW),
        out_shape=jax.ShapeDtypeStruct((SSD_GROUPS, m, SSD_GW), BF16),
        compiler_params=pltpu.CompilerParams(
            dimension_semantics=("arbitrary", "arbitrary"), vmem_limit_bytes=VMEM_LIMIT),
        name="ssd",
    )(xs, bm, cm, zg, dt, shift_table, per_group(conv_w), per_group(conv_b.reshape(1, SSD_XBC)), dtb,
      per_lane(a_log), per_lane(d_skip), norm_w.reshape(SSD_GROUPS, 1, SSD_GW))


def _mem_kernel(mem_ref, mnw_ref, wkv_ref, q_ref, g_ref, o_ref, *, q_chunk):
    t_len = q_ref.shape[1]
    mem = mem_ref[...]
    ms = jnp.mean(mem * mem, axis=-1, keepdims=True)
    mem_n = (mem * lax.rsqrt(ms + EPS) * mnw_ref[...]).astype(BF16)
    kv = jnp.dot(mem_n, wkv_ref[...], preferred_element_type=F32)
    k_all = (kv[:, :MEM_W] * (MEM_HD ** -0.5 * LOG2E)).astype(BF16)
    v_all = kv[:, MEM_W:].astype(BF16)
    def task(h, r0):
        k = k_all[:, h * MEM_HD:(h + 1) * MEM_HD]
        v = v_all[:, h * MEM_HD:(h + 1) * MEM_HD]
        s = lax.dot_general(q_ref[h, r0:r0 + q_chunk, :], k, _NT, preferred_element_type=F32)
        yield
        m = jnp.max(s, axis=-1, keepdims=True)
        p = jnp.exp2(s - m)
        l = jnp.sum(p, axis=-1, keepdims=True)
        yield
        o = jnp.dot(p.astype(BF16), v, preferred_element_type=F32) / l
        g = g_ref[h, r0:r0 + q_chunk, :].astype(F32)
        o_ref[h, r0:r0 + q_chunk, :] = (o * g).astype(BF16)

    _pipeline([task(h, r0) for h in range(MEM_HEADS) for r0 in range(0, t_len, q_chunk)])


def _mem_attn(mem2d, mem_norm_w, w_kv, qm, gm, batch, t_len, mem_len):
    m = batch * t_len
    return pl.pallas_call(
        functools.partial(_mem_kernel, q_chunk=512),
        grid=(batch,),
        in_specs=[pl.BlockSpec((mem_len, D_MODEL), lambda b: (b, 0)),
                  pl.BlockSpec((1, D_MODEL), lambda b: (0, 0)),
                  pl.BlockSpec((D_MODEL, 2 * MEM_W), lambda b: (0, 0), pipeline_mode=pl.Buffered(1)),
                  pl.BlockSpec((MEM_HEADS, t_len, MEM_HD), lambda b: (0, b, 0)),
                  pl.BlockSpec((MEM_HEADS, t_len, MEM_HD), lambda b: (0, b, 0))],
        out_specs=pl.BlockSpec((MEM_HEADS, t_len, MEM_HD), lambda b: (0, b, 0)),
        out_shape=jax.ShapeDtypeStruct((MEM_HEADS, m, MEM_HD), BF16),
        compiler_params=pltpu.CompilerParams(
            dimension_semantics=("arbitrary",), vmem_limit_bytes=VMEM_LIMIT),
        name="mem_attn",
    )(mem2d, mem_norm_w, w_kv, qm, gm)


def _outproj_kernel(oatt_ref, ossd_ref, omem_ref, x_ref, w_ref, fnw_ref, o_ref):
    tm = x_ref.shape[0]
    n_sub = tm // PROJ_SUB
    fnw = fnw_ref[...]
    refs = ([(oatt_ref, i) for i in range(ATT_PAIRS)] + [(ossd_ref, i) for i in range(SSD_GROUPS)]
            + [(omem_ref, i) for i in range(MEM_HEADS)])
    h_prev = None
    for sb in range(n_sub + 1):
        if sb < n_sub:
            rows = slice(sb * PROJ_SUB, (sb + 1) * PROJ_SUB)
            mixed = jnp.concatenate([ref[i, rows, :] for ref, i in refs], axis=-1)
            h = x_ref[rows, :] + jnp.dot(mixed, w_ref[...], preferred_element_type=F32)
        if h_prev is not None:
            ms = jnp.mean(h_prev * h_prev, axis=-1, keepdims=True)
            o_ref[(sb - 1) * PROJ_SUB:sb * PROJ_SUB, :] = h_prev * lax.rsqrt(ms + EPS) * fnw
        h_prev = h


def _out_proj(oatt, ossd, omem, x2d, w_out, final_norm_w, tm):
    m = x2d.shape[0]
    lead = lambda i: (0, i, 0)
    return pl.pallas_call(
        _outproj_kernel,
        grid=(m // tm,),
        in_specs=[pl.BlockSpec((ATT_PAIRS, tm, LANES), lead),
                  pl.BlockSpec((SSD_GROUPS, tm, SSD_GW), lead),
                  pl.BlockSpec((MEM_HEADS, tm, MEM_HD), lead),
                  pl.BlockSpec((tm, D_MODEL), lambda i: (i, 0)),
                  pl.BlockSpec((MIX_W, D_MODEL), lambda i: (0, 0), pipeline_mode=pl.Buffered(1)),
                  pl.BlockSpec((1, D_MODEL), lambda i: (0, 0))],
        out_specs=pl.BlockSpec((tm, D_MODEL), lambda i: (i, 0)),
        out_shape=jax.ShapeDtypeStruct((m, D_MODEL), F32),
        compiler_params=pltpu.CompilerParams(
            dimension_semantics=("arbitrary",), vmem_limit_bytes=VMEM_LIMIT),
        name="out_proj",
    )(oatt, ossd, omem, x2d, w_out, final_norm_w)


def kernel(x, mem, norm_w, w_in, conv_w, conv_b, dt_bias, a_log, d_skip, ssd_norm_w,
           mem_norm_w, w_mem_kv, w_out, final_norm_w):
    batch, t_len, d_model = x.shape
    mem_len = mem.shape[1]
    depth = norm_w.shape[0]
    assert d_model == D_MODEL and depth == 1 and t_len % MOBA_BLOCK == 0 and t_len // MOBA_BLOCK == 8
    x2d = x.reshape(batch * t_len, D_MODEL)
    mem2d = mem.reshape(batch * mem_len, D_MODEL)
    w = w_in[0]
    w_bf = w.astype(BF16)
    w_b = w_bf[:, DT_COL0 + SSD_HEADS:]
    w_dt = jnp.pad(w_bf[:, DT_COL0:DT_COL0 + SSD_HEADS], ((0, 0), (0, LANES - SSD_HEADS)))

    qkvg, zg, xs, bm, cm, qm, gm, dt = _in_proj(x2d, norm_w[0].reshape(1, D_MODEL), w_bf, w_b, w_dt, tm=PROJ_TILE)
    o_att = _moba(qkvg, _moba_key_table(t_len), batch, t_len)
    o_ssd = _ssd(xs, bm, cm, zg, dt, conv_w[0], conv_b[0], dt_bias[0], a_log[0], d_skip[0],
                 ssd_norm_w[0], batch, t_len)
    o_mem = _mem_attn(mem2d, mem_norm_w[0].reshape(1, D_MODEL), w_mem_kv[0].astype(BF16), qm, gm,
                      batch, t_len, mem_len)
    out = _out_proj(o_att, o_ssd, o_mem, x2d, w_out[0].astype(BF16), final_norm_w.reshape(1, D_MODEL), tm=PROJ_TILE)
    return out.reshape(batch, t_len, D_MODEL)
```

```python
import functools

import jax
import jax.numpy as jnp
import numpy as np
from jax import lax
from jax.experimental import pallas as pl
from jax.experimental.pallas import tpu as pltpu

F32 = jnp.float32
BF16 = jnp.bfloat16

D_MODEL = 1024
ATT_HEADS = 8
ATT_HD = 64
ATT_W = ATT_HEADS * ATT_HD
ATT_PAIRS = ATT_HEADS // 2
MOBA_BLOCK = 256
MOBA_TOPK = 3
MOBA_PAIRS_PER_STEP = 2
SSD_W = 1024
SSD_HD = 64
SSD_HEADS = 16
SSD_GROUPS = 4
SSD_HPG = SSD_HEADS // SSD_GROUPS
SSD_GW = SSD_W // SSD_GROUPS
SSD_STREAMS = 4
SSD_STATE = 128
SSD_CONV = 4
SSD_CHUNK = 128
SSD_XBC = SSD_W + 2 * SSD_GROUPS * SSD_STATE
MEM_HEADS = 4
MEM_HD = 128
MEM_W = MEM_HEADS * MEM_HD
MIX_W = ATT_W + SSD_W + MEM_W
EPS = 1e-6
NEG = -1e30
LOG2E = 1.4426950408889634
_LOG2E_TERMS = (1.4453125, -0.00262451171875, 7.063150405883789e-06)
LANES = 128
PROJ_TILE = 1024
PROJ_SUB = 256
DT_COL0 = 4 * ATT_W + SSD_W + SSD_XBC
VMEM_LIMIT = 56 * 1024 * 1024

_NT = (((1,), (1,)), ((), ()))


def _split2(a):
    hi = a.astype(BF16)
    lo = (a - hi.astype(F32)).astype(BF16)
    return hi, lo


def _silu(a):
    return a * jax.nn.sigmoid(a)


def _pipeline(gens):
    live = []
    pending = list(gens)
    while pending or live:
        if pending:
            live.append(pending.pop(0))
        for g in list(reversed(live)):
            try:
                next(g)
            except StopIteration:
                live.remove(g)


def _round_robin(gens):
    results = [None] * len(gens)
    active = list(range(len(gens)))
    while active:
        for j in list(active):
            try:
                next(gens[j])
            except StopIteration as done:
                results[j] = done.value
                active.remove(j)
    return tuple(results)


def _inproj_kernel(x_ref, nw_ref, wa_ref, wb_ref, wdt_ref,
                   qkvg_ref, zg_ref, xs_ref, bm_ref, cm_ref, qm_ref, gm_ref, dt_ref):
    tm = x_ref.shape[0]
    n_sub = tm // PROJ_SUB
    nw = nw_ref[...]

    def normed(sb):
        x = x_ref[sb * PROJ_SUB:(sb + 1) * PROJ_SUB, :]
        ms = jnp.mean(x * x, axis=-1, keepdims=True)
        return (x * lax.rsqrt(ms + EPS) * nw).astype(BF16)

    u_next = normed(0)
    for sb in range(n_sub):
        u = u_next
        if sb + 1 < n_sub:
            u_next = normed(sb + 1)
        rows = slice(sb * PROJ_SUB, (sb + 1) * PROJ_SUB)

        def scatter(ref, w_ref, c0, piece, count, first=0, post=None):
            for c in range(0, piece * count, 512):
                r = jnp.dot(u, w_ref[:, c0 + c:c0 + c + 512], preferred_element_type=F32)
                if post is not None:
                    r = post(r)
                r = r.astype(BF16)
                for j in range(512 // piece):
                    ref[first + (c + j * piece) // piece, rows, :] = r[:, j * piece:(j + 1) * piece]

        scatter(qkvg_ref, wa_ref, 0, LANES, ATT_PAIRS, post=lambda r: r * (ATT_HD ** -0.5 * LOG2E))
        scatter(qkvg_ref, wa_ref, ATT_W, LANES, 2 * ATT_PAIRS, first=ATT_PAIRS)
        scatter(qkvg_ref, wa_ref, 3 * ATT_W, LANES, ATT_PAIRS, first=3 * ATT_PAIRS, post=_silu)
        scatter(zg_ref, wa_ref, 4 * ATT_W, SSD_GW, SSD_GROUPS, post=_silu)
        scatter(xs_ref, wa_ref, 4 * ATT_W + SSD_W, SSD_GW, SSD_GROUPS)
        scatter(bm_ref, wa_ref, 4 * ATT_W + 2 * SSD_W, SSD_STATE, SSD_GROUPS)
        scatter(cm_ref, wa_ref, 4 * ATT_W + 2 * SSD_W + SSD_GROUPS * SSD_STATE, SSD_STATE, SSD_GROUPS)
        scatter(qm_ref, wb_ref, 0, MEM_HD, MEM_HEADS)
        scatter(gm_ref, wb_ref, MEM_W, MEM_HD, MEM_HEADS, post=_silu)
        dt_ref[rows, :] = jnp.dot(u, wdt_ref[...], preferred_element_type=F32)


def _in_proj(x2d, norm_w, w_a, w_b, w_dt, tm):
    m = x2d.shape[0]
    const = lambda i: (0, 0)
    lead = lambda i: (0, i, 0)
    out_shape = (
        jax.ShapeDtypeStruct((16, m, LANES), BF16),
        jax.ShapeDtypeStruct((SSD_GROUPS, m, SSD_GW), BF16),
        jax.ShapeDtypeStruct((SSD_GROUPS, m, SSD_GW), BF16),
        jax.ShapeDtypeStruct((SSD_GROUPS, m, SSD_STATE), BF16),
        jax.ShapeDtypeStruct((SSD_GROUPS, m, SSD_STATE), BF16),
        jax.ShapeDtypeStruct((MEM_HEADS, m, MEM_HD), BF16),
        jax.ShapeDtypeStruct((MEM_HEADS, m, MEM_HD), BF16),
        jax.ShapeDtypeStruct((m, LANES), F32),
    )
    out_specs = (
        pl.BlockSpec((16, tm, LANES), lead),
        pl.BlockSpec((SSD_GROUPS, tm, SSD_GW), lead),
        pl.BlockSpec((SSD_GROUPS, tm, SSD_GW), lead),
        pl.BlockSpec((SSD_GROUPS, tm, SSD_STATE), lead),
        pl.BlockSpec((SSD_GROUPS, tm, SSD_STATE), lead),
        pl.BlockSpec((MEM_HEADS, tm, MEM_HD), lead),
        pl.BlockSpec((MEM_HEADS, tm, MEM_HD), lead),
        pl.BlockSpec((tm, LANES), lambda i: (i, 0)),
    )
    return pl.pallas_call(
        _inproj_kernel,
        grid=(m // tm,),
        in_specs=[
            pl.BlockSpec((tm, D_MODEL), lambda i: (i, 0)),
            pl.BlockSpec((1, D_MODEL), const),
            pl.BlockSpec((D_MODEL, DT_COL0), const, pipeline_mode=pl.Buffered(1)),
            pl.BlockSpec((D_MODEL, 2 * MEM_W), const, pipeline_mode=pl.Buffered(1)),
            pl.BlockSpec((D_MODEL, LANES), const, pipeline_mode=pl.Buffered(1)),
        ],
        out_specs=out_specs,
        out_shape=out_shape,
        compiler_params=pltpu.CompilerParams(
            dimension_semantics=("arbitrary",), vmem_limit_bytes=VMEM_LIMIT),
        name="in_proj",
    )(x2d, norm_w, w_a, w_b, w_dt)


def _moba_pair_tasks(q_ref, k_ref, v_ref, g_ref, kc_ref, qc_ref, o_ref):
    t_len = q_ref.shape[0]
    nb = t_len // MOBA_BLOCK
    half = LANES // 2
    t_sel = min(MOBA_TOPK + 1, nb) * MOBA_BLOCK
    n_sel = t_len - t_sel

    k = k_ref[...]
    kc = kc_ref[...]
    lo = lax.broadcasted_iota(jnp.int32, (t_len, LANES), 1) < half
    ka = (jnp.where(lo, k, kc), jnp.where(lo, kc, k))
    lo_blk = lax.broadcasted_iota(jnp.int32, (MOBA_BLOCK, LANES), 1) < half
    key_i = lax.broadcasted_iota(jnp.int32, (MOBA_BLOCK, MOBA_BLOCK), 0)
    query_i = lax.broadcasted_iota(jnp.int32, (MOBA_BLOCK, MOBA_BLOCK), 1)
    causal_t = key_i <= query_i
    memo = {}

    def selection_lanes():
        if "aug" in memo:
            return memo["aug"]
        kmean = jnp.mean(k.astype(F32).reshape(nb, MOBA_BLOCK, LANES), axis=1)
        lo8 = lax.broadcasted_iota(jnp.int32, (nb, LANES), 1) < half
        terms = []
        for km in (jnp.where(lo8, kmean, 0.0), jnp.where(lo8, 0.0, kmean)):
            hi = km.astype(BF16).astype(F32)
            terms += [hi, km - hi]
        kstack = jnp.concatenate(terms, axis=0).astype(BF16)
        gate_t = lax.dot_general(kstack, q_ref[...], _NT, preferred_element_type=F32)[:, t_sel:]

        qblk = (lax.broadcasted_iota(jnp.int32, (nb, n_sel), 1) + t_sel) // MOBA_BLOCK
        jidx = lax.broadcasted_iota(jnp.int32, (nb, n_sel), 0)

        def select_bias(gate):
            cnt = jnp.zeros((nb, n_sel), jnp.int32)
            for i in range(nb):
                gi = gate[i:i + 1, :]
                beats = (gi > gate) | ((gi == gate) & (i < jidx))
                cnt = cnt + jnp.where(beats & (i < qblk), 1, 0)
            sel = ((cnt < MOBA_TOPK) & (jidx < qblk)) | (jidx == qblk)
            return jnp.where(sel, 0.0, NEG)

        bias0 = select_bias(gate_t[0:nb] + gate_t[nb:2 * nb])
        bias1 = select_bias(gate_t[2 * nb:3 * nb] + gate_t[3 * nb:4 * nb])
        crow = lax.broadcasted_iota(jnp.int32, (8, n_sel), 0)
        l2e = jnp.zeros((8, n_sel), F32)
        for i, term in enumerate(_LOG2E_TERMS + _LOG2E_TERMS):
            l2e = jnp.where(crow == i, term, l2e)
        pad = jnp.zeros((half - nb - 8, n_sel), F32)
        memo["aug"] = jnp.concatenate([bias1, l2e, pad, bias0, l2e, pad], axis=0).T.astype(BF16)
        return memo["aug"]

    def values_t():
        if "v" not in memo:
            v_t = v_ref[...].astype(F32).T.astype(BF16)
            ones_rows = (lax.broadcasted_iota(jnp.int32, (16, t_len), 0) == 0).astype(BF16)
            memo["v"] = [jnp.concatenate([v_t[h * ATT_HD:(h + 1) * ATT_HD], ones_rows], axis=0) for h in range(2)]
        return memo["v"]

    def head_task(n, h, sink):
        r0 = n * MOBA_BLOCK
        width = r0 + MOBA_BLOCK
        aug = qc_ref[...] if r0 < t_sel else selection_lanes()[r0 - t_sel:width - t_sel]
        q_blk = q_ref[r0:width, :]
        qa = jnp.where(lo_blk, q_blk, aug) if h == 0 else jnp.where(lo_blk, aug, q_blk)
        s = lax.dot_general(ka[h][:width], qa, _NT, preferred_element_type=F32)
        yield
        s_diag = jnp.where(causal_t, s[r0:], NEG)
        s = s_diag if n == 0 else jnp.concatenate([s[:r0], s_diag], axis=0)
        m = jnp.max(s, axis=0, keepdims=True)
        yield
        p = jnp.exp2(s - m).astype(BF16)
        yield
        o_t = jnp.dot(values_t()[h][:, :width], p, preferred_element_type=F32)
        sink[h] = o_t[:ATT_HD] / o_t[ATT_HD:ATT_HD + 1]
        if h == 1:
            o = jnp.concatenate([sink[0], sink[1]], axis=0).T
            g = g_ref[r0:width, :].astype(F32)
            o_ref[r0:width, :] = (o * g).astype(BF16)

    sinks = [dict() for _ in range(nb)]
    return [head_task(n, h, sinks[n]) for n in range(nb) for h in range(2)]


def _moba_kernel(q_ref, k_ref, v_ref, g_ref, kc_ref, qc_ref, o_ref):
    per_pair = [_moba_pair_tasks(q_ref.at[i], k_ref.at[i], v_ref.at[i], g_ref.at[i], kc_ref.at[i], qc_ref, o_ref.at[i])
                for i in range(MOBA_PAIRS_PER_STEP)]
    tasks = []
    for t in range(0, len(per_pair[0]), 2):
        for pair in per_pair:
            tasks += pair[t:t + 2]
    _pipeline(tasks)


def _moba(qkvg, kc, batch, t_len):
    m = batch * t_len
    qc = np.zeros((MOBA_BLOCK, LANES), np.float32)
    for i, term in enumerate(_LOG2E_TERMS + _LOG2E_TERMS):
        qc[:, 8 + i] = term
        qc[:, LANES // 2 + 8 + i] = term
    qc = jnp.asarray(qc, BF16)
    per_step = MOBA_PAIRS_PER_STEP
    steps = ATT_PAIRS // per_step
    blk = lambda kind: pl.BlockSpec((per_step, t_len, LANES), lambda b, hp: (kind * steps + hp, b, 0))
    return pl.pallas_call(
        _moba_kernel,
        grid=(batch, steps),
        in_specs=[blk(0), blk(1), blk(2), blk(3),
                  pl.BlockSpec((per_step, t_len, LANES), lambda b, hp: (hp, 0, 0)),
                  pl.BlockSpec((MOBA_BLOCK, LANES), lambda b, hp: (0, 0))],
        out_specs=pl.BlockSpec((per_step, t_len, LANES), lambda b, hp: (hp, b, 0)),
        out_shape=jax.ShapeDtypeStruct((ATT_PAIRS, m, LANES), BF16),
        compiler_params=pltpu.CompilerParams(
            dimension_semantics=("arbitrary", "arbitrary"), vmem_limit_bytes=VMEM_LIMIT),
        name="moba",
    )(qkvg, qkvg, qkvg, qkvg, kc, qc)


def _moba_key_table(t_len):
    pos = np.arange(t_len)
    blk = pos // MOBA_BLOCK
    off = pos % MOBA_BLOCK
    slopes = np.exp2(-8.0 * np.arange(1, ATT_HEADS + 1, dtype=np.float32) / ATT_HEADS)
    onehot = (blk[:, None] == np.arange(8)[None, :]).astype(np.float32)
    half = LANES // 2

    def head_table(h):
        in_blk = (slopes[h] * off.astype(np.float32))[:, None]
        blk_start = (slopes[h] * MOBA_BLOCK * blk.astype(np.float32))[:, None]
        cols = [onehot, in_blk, in_blk, in_blk, blk_start, blk_start, blk_start,
                np.zeros((t_len, half - 14), np.float32)]
        return np.concatenate(cols, axis=1)

    tables = [np.concatenate([head_table(2 * hp + 1), head_table(2 * hp)], axis=1)
              for hp in range(ATT_PAIRS)]
    return jnp.asarray(np.stack(tables), BF16)


def _ssd_kernel(xs_ref, bm_ref, cm_ref, zg_ref, dt_ref, shift_ref, cw_ref, cb_ref,
                dtb_ref, alog_ref, dskip_ref, nw_ref, o_ref):
    t_len = xs_ref.shape[1]
    nchunks = t_len // SSD_CHUNK
    conv_width = SSD_GW + 2 * SSD_STATE
    halo_rows = 8

    dtb = dtb_ref[...]
    hrow = lax.broadcasted_iota(jnp.int32, (LANES, SSD_GW), 0)
    lcol = lax.broadcasted_iota(jnp.int32, (LANES, SSD_GW), 1)
    li = lax.broadcasted_iota(jnp.int32, (SSD_CHUNK, SSD_CHUNK), 0)
    si = lax.broadcasted_iota(jnp.int32, (SSD_CHUNK, SSD_CHUNK), 1)
    tri = li >= si
    lmat = tri.astype(BF16)
    head_of_lane = lax.broadcasted_iota(jnp.int32, (SSD_CHUNK, SSD_GW), 1) // SSD_HD

    a_es = [-jnp.exp(alog_ref[j]) * LOG2E for j in range(SSD_STREAMS)]
    expands = [(hrow == SSD_HPG * (pl.program_id(1) * SSD_STREAMS + j) + lcol // SSD_HD).astype(BF16)
               for j in range(SSD_STREAMS)]

    def stream_chunk(j, rows, h_t, halo):
        cw = cw_ref[j]
        a_e, expand = a_es[j], expands[j]

        dt = jnp.logaddexp(dt_ref[rows, :] + dtb, 0.0)
        dte = None
        for term in _split2(dt):
            part = jnp.dot(term, expand, preferred_element_type=F32)
            dte = part if dte is None else dte + part
        yield
        raw = jnp.concatenate([xs_ref[j, rows, :], bm_ref[j, rows, :], cm_ref[j, rows, :]], axis=1)
        shifted = jnp.dot(shift_ref[...], raw, preferred_element_type=F32)
        raw_f = raw.astype(F32)
        da = dte * a_e
        cs = None
        for term in _split2(da):
            part = jnp.dot(lmat, term, preferred_element_type=F32)
            cs = part if cs is None else cs + part
        zgate = zg_ref[j, rows, :].astype(F32)
        yield
        cs_t = cs.T
        cs_last = cs[SSD_CHUNK - 1:SSD_CHUNK, :]
        to_end = jnp.exp2(cs_last - cs)
        from_start = jnp.exp2(cs)
        yield
        hwin = jnp.concatenate([halo, jnp.zeros_like(halo)], axis=0)
        acc = cb_ref[j] + raw_f * cw[SSD_CONV - 1:SSD_CONV, :]
        edge = None
        for s in range(1, SSD_CONV):
            tap = cw[SSD_CONV - 1 - s:SSD_CONV - s, :]
            acc = acc + shifted[(s - 1) * SSD_CHUNK:s * SSD_CHUNK] * tap
            part = pltpu.roll(hwin, s, axis=0)[halo_rows:] * tap
            edge = part if edge is None else edge + part
        acc = jnp.concatenate([acc[:halo_rows] + edge, acc[halo_rows:]], axis=0)
        xbc = _silu(acc)
        xs = xbc[:, :SSD_GW]
        bm = xbc[:, SSD_GW:SSD_GW + SSD_STATE]
        cm = xbc[:, SSD_GW + SSD_STATE:]
        xdt = xs * dte
        xdt_b = xdt.astype(BF16)
        cm_b = cm.astype(BF16)
        cb = lax.dot_general(cm_b, bm.astype(BF16), _NT, preferred_element_type=F32)
        y_off = jnp.dot(cm_b, h_t.astype(BF16), preferred_element_type=F32)
        st = jnp.dot(bm.T.astype(BF16), (xdt * to_end).astype(BF16), preferred_element_type=F32)
        yield
        ws = []
        for k in range(SSD_HPG):
            colb = jnp.broadcast_to(cs[:, SSD_HD * k:SSD_HD * k + 1], (SSD_CHUNK, SSD_CHUNK))
            rowb = cs_t[SSD_HD * k:SSD_HD * k + 1, :]
            ws.append((cb * jnp.exp2(jnp.where(tri, colb - rowb, NEG))).astype(BF16))
        xm = [jnp.where(head_of_lane == k, xdt_b, jnp.zeros_like(xdt_b)) for k in range(SSD_HPG)]
        y = jnp.dot(jnp.concatenate(ws, axis=1), jnp.concatenate(xm, axis=0),
                    preferred_element_type=F32)
        yield
        y = y + y_off * from_start + dskip_ref[j] * xs
        gated = y * zgate
        var = jnp.mean(gated * gated, axis=-1, keepdims=True)
        o_ref[j, rows, :] = (gated * lax.rsqrt(var + EPS) * nw_ref[j]).astype(BF16)
        return h_t * jnp.exp2(cs_last) + st, raw_f[SSD_CHUNK - halo_rows:, :]

    def chunk(c, carry):
        rows = pl.ds(pl.multiple_of(c * SSD_CHUNK, SSD_CHUNK), SSD_CHUNK)
        return _round_robin([stream_chunk(j, rows, *carry[j]) for j in range(SSD_STREAMS)])

    init = (jnp.zeros((SSD_STATE, SSD_GW), F32), jnp.zeros((halo_rows, conv_width), F32))
    lax.fori_loop(0, nchunks, chunk, (init,) * SSD_STREAMS)


def _ssd(xs, bm, cm, zg, dt, conv_w, conv_b, dt_bias, a_log, d_skip, norm_w, batch, t_len):
    m = batch * t_len
    g_blk = lambda w: pl.BlockSpec((SSD_STREAMS, t_len, w), lambda b, g: (g, b, 0))
    g_par = lambda r, w: pl.BlockSpec((SSD_STREAMS, r, w), lambda b, g: (g, 0, 0))
    bc_w = SSD_GROUPS * SSD_STATE

    def per_group(a):
        r = a.shape[0]
        return jnp.concatenate([a[:, :SSD_W].reshape(r, SSD_GROUPS, SSD_GW),
                                a[:, SSD_W:SSD_W + bc_w].reshape(r, SSD_GROUPS, SSD_STATE),
                                a[:, SSD_W + bc_w:].reshape(r, SSD_GROUPS, SSD_STATE)], axis=2).transpose(1, 0, 2)

    dtb = jnp.pad(dt_bias, (0, LANES - SSD_HEADS)).reshape(1, LANES)
    per_lane = lambda a: jnp.repeat(a, SSD_HD).reshape(SSD_GROUPS, 1, SSD_GW)
    conv_width = SSD_GW + 2 * SSD_STATE
    t_out = np.arange((SSD_CONV - 1) * SSD_CHUNK)[:, None]
    shift_table = jnp.asarray(t_out % SSD_CHUNK - (t_out // SSD_CHUNK + 1) == np.arange(SSD_CHUNK)[None, :], BF16)
    return pl.pallas_call(
        _ssd_kernel,
        grid=(batch, SSD_GROUPS // SSD_STREAMS),
        in_specs=[g_blk(SSD_GW), g_blk(SSD_STATE), g_blk(SSD_STATE), g_blk(SSD_GW),
                  pl.BlockSpec((t_len, LANES), lambda b, g: (b, 0)),
                  pl.BlockSpec(((SSD_CONV - 1) * SSD_CHUNK, SSD_CHUNK), lambda b, g: (0, 0)),
                  g_par(SSD_CONV, conv_width), g_par(1, conv_width),
                  pl.BlockSpec((1, LANES), lambda b, g: (0, 0)),
                  g_par(1, SSD_GW), g_par(1, SSD_GW), g_par(1, SSD_GW)],
        out_specs=g_blk(SSD_GW),
        out_shape=jax.ShapeDtypeStruct((SSD_GROUPS, m, SSD_GW), BF16),
        compiler_params=pltpu.CompilerParams(
            dimension_semantics=("arbitrary", "arbitrary"), vmem_limit_bytes=VMEM_LIMIT),
        name="ssd",
    )(xs, bm, cm, zg, dt, shift_table, per_group(conv_w), per_group(conv_b.reshape(1, SSD_XBC)), dtb,
      per_lane(a_log), per_lane(d_skip), norm_w.reshape(SSD_GROUPS, 1, SSD_GW))


def _mem_kernel(mem_ref, mnw_ref, wkv_ref, q_ref, g_ref, o_ref, *, q_chunk):
    t_len = q_ref.shape[1]
    mem = mem_ref[...]
    ms = jnp.mean(mem * mem, axis=-1, keepdims=True)
    mem_n = (mem * lax.rsqrt(ms + EPS) * mnw_ref[...]).astype(BF16)
    kv = jnp.dot(mem_n, wkv_ref[...], preferred_element_type=F32)
    k_all = (kv[:, :MEM_W] * (MEM_HD ** -0.5 * LOG2E)).astype(BF16)
    v_all = kv[:, MEM_W:].astype(BF16)
    def task(h, r0):
        k = k_all[:, h * MEM_HD:(h + 1) * MEM_HD]
        v = v_all[:, h * MEM_HD:(h + 1) * MEM_HD]
        s = lax.dot_general(q_ref[h, r0:r0 + q_chunk, :], k, _NT, preferred_element_type=F32)
        yield
        m = jnp.max(s, axis=-1, keepdims=True)
        p = jnp.exp2(s - m)
        l = jnp.sum(p, axis=-1, keepdims=True)
        yield
        o = jnp.dot(p.astype(BF16), v, preferred_element_type=F32) / l
        g = g_ref[h, r0:r0 + q_chunk, :].astype(F32)
        o_ref[h, r0:r0 + q_chunk, :] = (o * g).astype(BF16)

    _pipeline([task(h, r0) for h in range(MEM_HEADS) for r0 in range(0, t_len, q_chunk)])


def _mem_attn(mem2d, mem_norm_w, w_kv, qm, gm, batch, t_len, mem_len):
    m = batch * t_len
    return pl.pallas_call(
        functools.partial(_mem_kernel, q_chunk=512),
        grid=(batch,),
        in_specs=[pl.BlockSpec((mem_len, D_MODEL), lambda b: (b, 0)),
                  pl.BlockSpec((1, D_MODEL), lambda b: (0, 0)),
                  pl.BlockSpec((D_MODEL, 2 * MEM_W), lambda b: (0, 0), pipeline_mode=pl.Buffered(1)),
                  pl.BlockSpec((MEM_HEADS, t_len, MEM_HD), lambda b: (0, b, 0)),
                  pl.BlockSpec((MEM_HEADS, t_len, MEM_HD), lambda b: (0, b, 0))],
        out_specs=pl.BlockSpec((MEM_HEADS, t_len, MEM_HD), lambda b: (0, b, 0)),
        out_shape=jax.ShapeDtypeStruct((MEM_HEADS, m, MEM_HD), BF16),
        compiler_params=pltpu.CompilerParams(
            dimension_semantics=("arbitrary",), vmem_limit_bytes=VMEM_LIMIT),
        name="mem_attn",
    )(mem2d, mem_norm_w, w_kv, qm, gm)


def _outproj_kernel(oatt_ref, ossd_ref, omem_ref, x_ref, w_ref, fnw_ref, o_ref):
    tm = x_ref.shape[0]
    n_sub = tm // PROJ_SUB
    fnw = fnw_ref[...]
    refs = ([(oatt_ref, i) for i in range(ATT_PAIRS)] + [(ossd_ref, i) for i in range(SSD_GROUPS)]
            + [(omem_ref, i) for i in range(MEM_HEADS)])
    h_prev = None
    for sb in range(n_sub + 1):
        if sb < n_sub:
            rows = slice(sb * PROJ_SUB, (sb + 1) * PROJ_SUB)
            mixed = jnp.concatenate([ref[i, rows, :] for ref, i in refs], axis=-1)
            h = x_ref[rows, :] + jnp.dot(mixed, w_ref[...], preferred_element_type=F32)
        if h_prev is not None:
            ms = jnp.mean(h_prev * h_prev, axis=-1, keepdims=True)
            o_ref[(sb - 1) * PROJ_SUB:sb * PROJ_SUB, :] = h_prev * lax.rsqrt(ms + EPS) * fnw
        h_prev = h


def _out_proj(oatt, ossd, omem, x2d, w_out, final_norm_w, tm):
    m = x2d.shape[0]
    lead = lambda i: (0, i, 0)
    return pl.pallas_call(
        _outproj_kernel,
        grid=(m // tm,),
        in_specs=[pl.BlockSpec((ATT_PAIRS, tm, LANES), lead),
                  pl.BlockSpec((SSD_GROUPS, tm, SSD_GW), lead),
                  pl.BlockSpec((MEM_HEADS, tm, MEM_HD), lead),
                  pl.BlockSpec((tm, D_MODEL), lambda i: (i, 0)),
                  pl.BlockSpec((MIX_W, D_MODEL), lambda i: (0, 0), pipeline_mode=pl.Buffered(1)),
                  pl.BlockSpec((1, D_MODEL), lambda i: (0, 0))],
        out_specs=pl.BlockSpec((tm, D_MODEL), lambda i: (i, 0)),
        out_shape=jax.ShapeDtypeStruct((m, D_MODEL), F32),
        compiler_params=pltpu.CompilerParams(
            dimension_semantics=("arbitrary",), vmem_limit_bytes=VMEM_LIMIT),
        name="out_proj",
    )(oatt, ossd, omem, x2d, w_out, final_norm_w)


def kernel(x, mem, norm_w, w_in, conv_w, conv_b, dt_bias, a_log, d_skip, ssd_norm_w,
           mem_norm_w, w_mem_kv, w_out, final_norm_w):
    batch, t_len, d_model = x.shape
    mem_len = mem.shape[1]
    depth = norm_w.shape[0]
    assert d_model == D_MODEL and depth == 1 and t_len % MOBA_BLOCK == 0 and t_len // MOBA_BLOCK == 8
    x2d = x.reshape(batch * t_len, D_MODEL)
    mem2d = mem.reshape(batch * mem_len, D_MODEL)
    w = w_in[0]
    w_bf = w.astype(BF16)
    w_b = w_bf[:, DT_COL0 + SSD_HEADS:]
    w_dt = jnp.pad(w_bf[:, DT_COL0:DT_COL0 + SSD_HEADS], ((0, 0), (0, LANES - SSD_HEADS)))

    qkvg, zg, xs, bm, cm, qm, gm, dt = _in_proj(x2d, norm_w[0].reshape(1, D_MODEL), w_bf, w_b, w_dt, tm=PROJ_TILE)
    o_att = _moba(qkvg, _moba_key_table(t_len), batch, t_len)
    o_ssd = _ssd(xs, bm, cm, zg, dt, conv_w[0], conv_b[0], dt_bias[0], a_log[0], d_skip[0],
                 ssd_norm_w[0], batch, t_len)
    o_mem = _mem_attn(mem2d, mem_norm_w[0].reshape(1, D_MODEL), w_mem_kv[0].astype(BF16), qm, gm,
                      batch, t_len, mem_len)
    out = _out_proj(o_att, o_ssd, o_mem, x2d, w_out[0].astype(BF16), final_norm_w.reshape(1, D_MODEL), tm=PROJ_TILE)
    return out.reshape(batch, t_len, D_MODEL)
```
